```python
import jax
import jax.numpy as jnp
from jax import lax
import numpy as np

D_MODEL = 1024
BATCH = 2
SEQ = 8192
DEPTH = 4
DEC_BATCH = 128
DEC_SEQ = 1
PAST_LEN = 2048
PAGE_SIZE = 128

HEAD_DIM = 64
N_HEADS = 8
KV_HEADS = 4
D_BRANCH = N_HEADS * HEAD_DIM
KV_WIDTH = KV_HEADS * HEAD_DIM
N_BRANCH = 3
IDX_HEADS = 8
IDX_DIM = 64
TOPK_MAX = 256
Q_BLOCK = 128
ROPE_THETA = 10000.0
RMS_EPS = 1e-6
FORGET_BIAS_CENTER = 2.0

PROJ_SIZES = (
    D_BRANCH, KV_WIDTH, KV_WIDTH, D_BRANCH, N_HEADS,
    D_BRANCH, KV_WIDTH, KV_WIDTH, D_BRANCH, IDX_HEADS * IDX_DIM, IDX_DIM, IDX_HEADS,
    D_BRANCH, KV_WIDTH, KV_WIDTH, D_BRANCH,
    N_BRANCH * D_MODEL,
)
PROJ_SPLITS = tuple(np.cumsum(PROJ_SIZES)[:-1].tolist())
IN_COLS = int(sum(PROJ_SIZES))

kernel_name = "gated_fox_dsa_stickbreak_step"


def rms_norm(x, g):
    xf = x.astype(jnp.float32)
    y = xf * lax.rsqrt(jnp.mean(xf * xf, axis=-1, keepdims=True) + RMS_EPS)
    return (y * g.astype(jnp.float32)).astype(x.dtype)


def rope(x, pos):
    half = x.shape[-1] // 2
    inv_freq = ROPE_THETA ** (-jnp.arange(half, dtype=jnp.float32) / half)
    ang = pos.astype(jnp.float32)[:, None] * inv_freq[None, :]
    cos = jnp.cos(ang)[:, None, :]
    sin = jnp.sin(ang)[:, None, :]
    x1 = x[..., :half].astype(jnp.float32)
    x2 = x[..., half:].astype(jnp.float32)
    return jnp.concatenate([x1 * cos - x2 * sin, x2 * cos + x1 * sin], axis=-1).astype(x.dtype)


def to_blocks(a):
    b, s = a.shape[:2]
    return jnp.swapaxes(a.reshape(b, s // Q_BLOCK, Q_BLOCK, *a.shape[2:]), 0, 1)


def from_blocks(a):
    nb, b, q = a.shape[:3]
    return jnp.swapaxes(a, 0, 1).reshape(b, nb * q, *a.shape[3:])


def project(h, w_in, b_forget):
    b, t, _ = h.shape
    p = jnp.einsum('btd,dc->btc', h, w_in)
    (aq, ak, av, az, af, bq, bk, bv, bz, biq, bik, biw,
     cq, ck, cv, cz, gates) = jnp.split(p, PROJ_SPLITS, axis=-1)
    hd = lambda a, n: a.reshape(b, t, n, HEAD_DIM)
    return {
        'a_q': hd(aq, N_HEADS), 'a_k': hd(ak, KV_HEADS), 'a_v': hd(av, KV_HEADS), 'a_z': az,
        'a_logf': jax.nn.log_sigmoid((af + b_forget).astype(jnp.float32)),
        'b_q': hd(bq, N_HEADS), 'b_k': hd(bk, KV_HEADS), 'b_v': hd(bv, KV_HEADS), 'b_z': bz,
        'b_iq': biq.reshape(b, t, IDX_HEADS, IDX_DIM), 'b_ik': bik, 'b_iw': biw,
        'c_q': hd(cq, N_HEADS), 'c_k': hd(ck, KV_HEADS), 'c_v': hd(cv, KV_HEADS), 'c_z': cz,
        'gates': gates.reshape(b, t, N_BRANCH, D_MODEL),
    }


def fox_attend(q, k, v, d_q, d_k, q_pos, k_pos):
    b, t, h, dh = q.shape
    kvh = k.shape[2]
    g = h // kvh
    s = jnp.einsum('btkgd,bskd->bkgts', q.reshape(b, t, kvh, g, dh), k,
                   preferred_element_type=jnp.float32) * dh ** -0.5
    dq = jnp.moveaxis(d_q.reshape(b, t, kvh, g), 1, 3)[..., :, None]
    dk = jnp.moveaxis(d_k.reshape(b, -1, kvh, g), 1, 3)[..., None, :]
    mask = k_pos[None, :] <= q_pos[:, None]
    p = jax.nn.softmax(jnp.where(mask, s + (dq - dk), -jnp.inf), axis=-1)
    o = jnp.einsum('bkgts,bskd->btkgd', p.astype(v.dtype), v)
    return o.reshape(b, t, h * dh)


def sb_attend(q, k, v, q_pos, k_pos):
    b, t, h, dh = q.shape
    kvh = k.shape[2]
    g = h // kvh
    z = jnp.einsum('btkgd,bskd->bkgts', q.reshape(b, t, kvh, g, dh), k,
                   preferred_element_type=jnp.float32) * dh ** -0.5
    valid = k_pos[None, :] < q_pos[:, None]
    log_keep = jnp.where(valid, jax.nn.log_sigmoid(-z), 0.0)
    log_after = lax.cumsum(log_keep, axis=z.ndim - 1, reverse=True) - log_keep
    a = jnp.where(valid, jnp.exp(jax.nn.log_sigmoid(z) + log_after), 0.0)
    o = jnp.einsum('bkgts,bskd->btkgd', a.astype(v.dtype), v)
    return o.reshape(b, t, h * dh)


def dsa_select(iq, iw, ik, q_pos):
    n_keys = ik.shape[1]
    k_sel = max(1, min(TOPK_MAX, n_keys // 4))
    r = jax.nn.relu(jnp.einsum('bthd,bsd->bths', iq, ik,
                               preferred_element_type=jnp.float32) * IDX_DIM ** -0.5)
    score = jnp.einsum('bths,bth->bts', r, iw.astype(jnp.float32)) * IDX_HEADS ** -0.5
    k_pos = jnp.arange(n_keys)
    score = jnp.where(k_pos[None, None, :] <= q_pos[None, :, None], score, -jnp.inf)
    _, idx = lax.top_k(score, k_sel)
    valid = idx <= q_pos[None, :, None]
    return idx, valid


def sparse_attend(q, ks, vs, valid):
    b, t, h, dh = q.shape
    kvh = ks.shape[3]
    g = h // kvh
    s = jnp.einsum('btkgd,btjkd->btkgj', q.reshape(b, t, kvh, g, dh), ks,
                   preferred_element_type=jnp.float32) * dh ** -0.5
    p = jax.nn.softmax(jnp.where(valid[:, :, None, None, :], s, -jnp.inf), axis=-1)
    o = jnp.einsum('btkgj,btjkd->btkgd', p.astype(vs.dtype), vs)
    return o.reshape(b, t, h * dh)


def merge(p, oa, ob, oc, w_branch, w_out):
    o = jnp.stack([oa * jax.nn.silu(p['a_z']), ob * jax.nn.silu(p['b_z']),
                   oc * jax.nn.silu(p['c_z'])], axis=2)
    y = jnp.einsum('btne,ned->btnd', o, w_branch)
    m = jnp.sum(jax.nn.sigmoid(p['gates']) * y, axis=2)
    return jnp.einsum('btd,de->bte', m, w_out)


def layer_prompt(x, norm_g, w_in, b_forget, w_branch, w_out):
    b, s, _ = x.shape
    h = rms_norm(x, norm_g)
    p = project(h, w_in, b_forget)
    pos = jnp.arange(s)
    qpos_blk = pos.reshape(-1, Q_BLOCK)
    dcum = jnp.cumsum(p['a_logf'], axis=1)
    oa = from_blocks(lax.map(
        lambda a: fox_attend(a[0], p['a_k'], p['a_v'], a[1], dcum, a[2], pos),
        (to_blocks(p['a_q']), to_blocks(dcum), qpos_blk)))
    bq = rope(p['b_q'], pos)
    bk = rope(p['b_k'], pos)
    biq = rope(p['b_iq'], pos)
    bik = rope(p['b_ik'][:, :, None, :], pos)[:, :, 0]
    bv = p['b_v']
    bidx = jnp.arange(b)[:, None, None]

    def dsa_block(a):
        qb, iqb, iwb, qp = a
        idx, valid = dsa_select(iqb, iwb, bik, qp)
        return sparse_attend(qb, bk[bidx, idx], bv[bidx, idx], valid)

    ob = from_blocks(lax.map(dsa_block, (to_blocks(bq), to_blocks(biq),
                                         to_blocks(p['b_iw']), qpos_blk)))
    oc = from_blocks(lax.map(
        lambda a: sb_attend(a[0], p['c_k'], p['c_v'], a[1], pos),
        (to_blocks(p['c_q']), qpos_blk)))
    out = x + merge(p, oa, ob, oc, w_branch, w_out)
    new = (p['a_k'], p['a_v'], p['a_logf'], bk, bv, bik, p['c_k'], p['c_v'])
    return out, new


def layer_sample(x, l, page_table, ca_k, ca_v, ca_f, cb_k, cb_v, cb_ik, cc_k, cc_v,
                 norm_g, w_in, b_forget, w_branch, w_out):
    db, t, _ = x.shape
    page = ca_k.shape[2]
    past = page_table.shape[1] * page
    h = rms_norm(x, norm_g)
    p = project(h, w_in, b_forget)
    q_pos = past + jnp.arange(t)
    k_pos = jnp.arange(past + t)

    def past_rows(cache):
        r = cache[l, page_table]
        return r.reshape(db, past, *r.shape[3:])

    ka = jnp.concatenate([past_rows(ca_k), p['a_k']], axis=1)
    va = jnp.concatenate([past_rows(ca_v), p['a_v']], axis=1)
    logf = jnp.concatenate([past_rows(ca_f).astype(jnp.float32), p['a_logf']], axis=1)
    dcum = jnp.cumsum(logf, axis=1)
    oa = fox_attend(p['a_q'], ka, va, dcum[:, past:], dcum, q_pos, k_pos)
    bq = rope(p['b_q'], q_pos)
    bk = rope(p['b_k'], q_pos)
    biq = rope(p['b_iq'], q_pos)
    bik = rope(p['b_ik'][:, :, None, :], q_pos)[:, :, 0]
    bv = p['b_v']
    ik_all = jnp.concatenate([past_rows(cb_ik), bik], axis=1)
    idx, valid = dsa_select(biq, p['b_iw'], ik_all, q_pos)
    bidx = jnp.arange(db)[:, None, None]
    is_past = (idx < past)[..., None, None]
    pidx = jnp.minimum(idx, past - 1)
    phys = page_table[bidx, pidx // page]
    off = pidx % page
    nidx = jnp.clip(idx - past, 0, t - 1)
    ks = jnp.where(is_past, cb_k[l, phys, off], bk[bidx, nidx])
    vs = jnp.where(is_past, cb_v[l, phys, off], bv[bidx, nidx])
    ob = sparse_attend(bq, ks, vs, valid)
    kc = jnp.concatenate([past_rows(cc_k), p['c_k']], axis=1)
    vc = jnp.concatenate([past_rows(cc_v), p['c_v']], axis=1)
    oc = sb_attend(p['c_q'], kc, vc, q_pos, k_pos)
    out = x + merge(p, oa, ob, oc, w_branch, w_out)
    new = (p['a_k'], p['a_v'], p['a_logf'], bk, bv, bik, p['c_k'], p['c_v'])
    return out, new


def setup_inputs(seed: int = 0) -> dict:
    key = jax.random.key(seed)
    ks = jax.random.split(key, 20)
    n_pages = PAST_LEN // PAGE_SIZE
    n_used = DEC_BATCH * n_pages
    n_pool = n_used + max(1, n_used // 4)
    f32 = jnp.float32
    nrm = lambda k, shape, s=1.0: s * jax.random.normal(k, shape, f32)
    kv_shape = (DEPTH, n_pool, PAGE_SIZE, KV_HEADS, HEAD_DIM)
    page_table = jax.random.permutation(ks[10], n_pool)[:n_used].reshape(DEC_BATCH, n_pages).astype(jnp.int32)
    return {
        "x_prompt": nrm(ks[0], (BATCH, SEQ, D_MODEL)),
        "x_sample": nrm(ks[1], (DEC_BATCH, DEC_SEQ, D_MODEL)),
        "cache_a_k": nrm(ks[2], kv_shape),
        "cache_a_v": nrm(ks[3], kv_shape),
        "cache_a_logf": jax.nn.log_sigmoid(FORGET_BIAS_CENTER + nrm(ks[4], (DEPTH, n_pool, PAGE_SIZE, N_HEADS))),
        "cache_b_k": nrm(ks[5], kv_shape),
        "cache_b_v": nrm(ks[6], kv_shape),
        "cache_b_idx_k": nrm(ks[7], (DEPTH, n_pool, PAGE_SIZE, IDX_DIM)),
        "cache_c_k": nrm(ks[8], kv_shape),
        "cache_c_v": nrm(ks[9], kv_shape),
        "page_table": page_table,
        "norm_g": 1.0 + nrm(ks[11], (DEPTH, D_MODEL), 0.02),
        "w_in": nrm(ks[12], (DEPTH, D_MODEL, IN_COLS), D_MODEL ** -0.5),
        "b_forget": FORGET_BIAS_CENTER + nrm(ks[13], (DEPTH, N_HEADS), 0.1),
        "w_branch": nrm(ks[14], (DEPTH, N_BRANCH, D_BRANCH, D_MODEL), D_BRANCH ** -0.5),
        "w_out": nrm(ks[15], (DEPTH, D_MODEL, D_MODEL), D_MODEL ** -0.5),
        "final_g": 1.0 + nrm(ks[16], (D_MODEL,), 0.02),
    }


def reference(x_prompt, x_sample, cache_a_k, cache_a_v, cache_a_logf, cache_b_k, cache_b_v,
              cache_b_idx_k, cache_c_k, cache_c_v, page_table, norm_g, w_in, b_forget,
              w_branch, w_out, final_g):
    xp = x_prompt
    xs = x_sample
    p_rows = []
    s_rows = []
    for l in range(DEPTH):
        xp, pn = layer_prompt(xp, norm_g[l], w_in[l], b_forget[l], w_branch[l], w_out[l])
        xs, sn = layer_sample(xs, l, page_table, cache_a_k, cache_a_v, cache_a_logf,
                              cache_b_k, cache_b_v, cache_b_idx_k, cache_c_k, cache_c_v,
                              norm_g[l], w_in[l], b_forget[l], w_branch[l], w_out[l])
        p_rows.append(pn)
        s_rows.append(sn)
    y_prompt = rms_norm(xp, final_g)
    y_sample = rms_norm(xs, final_g)
    pa_k, pa_v, pa_logf, pb_k, pb_v, pb_idx_k, pc_k, pc_v = [jnp.stack([r[i] for r in p_rows]) for i in range(8)]
    sa_k, sa_v, sa_logf, sb_k, sb_v, sb_idx_k, sc_k, sc_v = [jnp.stack([r[i] for r in s_rows]) for i in range(8)]
    return (y_prompt, y_sample, pa_k, pa_v, pa_logf, pb_k, pb_v, pb_idx_k, pc_k, pc_v,
            sa_k, sa_v, sa_logf, sb_k, sb_v, sb_idx_k, sc_k, sc_v)
```

```python
import functools

import numpy as np
import jax
import jax.numpy as jnp
from jax import lax
from jax.experimental import pallas as pl
from jax.experimental.pallas import tpu as pltpu

F32 = jnp.float32
BF16 = jnp.bfloat16
I32 = jnp.int32

D_MODEL = 1024
HEAD_DIM = 64
N_HEADS = 8
KV_HEADS = 4
D_BRANCH = N_HEADS * HEAD_DIM
KV_WIDTH = KV_HEADS * HEAD_DIM
N_BRANCH = 3
IDX_HEADS = 8
IDX_DIM = 64
TOPK_MAX = 256
ROPE_THETA = 10000.0
RMS_EPS = 1e-6

PROJ_SIZES = (
    D_BRANCH, KV_WIDTH, KV_WIDTH, D_BRANCH, N_HEADS,
    D_BRANCH, KV_WIDTH, KV_WIDTH, D_BRANCH, IDX_HEADS * IDX_DIM, IDX_DIM, IDX_HEADS,
    D_BRANCH, KV_WIDTH, KV_WIDTH, D_BRANCH,
    N_BRANCH * D_MODEL,
)

LANES = 128
SUBLANES = 8
CUMSUM_CHUNK = 256
VMEM_LIMIT_BYTES = 56 * 1024 * 1024
NEG_BIG = -1e30
INT_MIN = -(2 ** 31)


def _cparams(*sem):
    return pltpu.CompilerParams(dimension_semantics=sem, vmem_limit_bytes=VMEM_LIMIT_BYTES)


def _softplus(x):
    return jnp.maximum(x, 0.0) + jnp.log1p(jnp.exp(-jnp.abs(x)))


def _dot(a, b):
    return jnp.dot(a, b, preferred_element_type=F32)


def _dot_nt(a, b):
    return lax.dot_general(a, b, (((1,), (1,)), ((), ())), preferred_element_type=F32)


def _dot_split(x, tri):
    hi = x.astype(BF16)
    lo = (x - hi.astype(F32)).astype(BF16)
    return _dot(hi, tri) + _dot(lo, tri)


def _sortable(x):
    x = jnp.where(x == 0.0, 0.0, x)
    bits = pltpu.bitcast(x, I32)
    return jnp.where(bits >= 0, bits, bits ^ 0x7FFFFFFF)


def _rms_bf16(x_ref, g_ref):
    xf = x_ref[...]
    ms = jnp.mean(xf * xf, axis=-1, keepdims=True)
    return (xf * lax.rsqrt(ms + RMS_EPS) * g_ref[...]).astype(BF16)


def _rope128(x, cos, sin):
    lane = lax.broadcasted_iota(I32, x.shape, 1)
    partner = jnp.where((lane & 32) == 0, pltpu.roll(x, 96, 1), pltpu.roll(x, 32, 1))
    return x * cos + partner * sin


def _pair_lower_upper(blk):
    lane = lax.broadcasted_iota(I32, blk.shape, 1)
    swapped = pltpu.roll(blk, 64, 1)
    low = lane < HEAD_DIM
    return jnp.where(low, blk, swapped), jnp.where(low, swapped, blk)


def _store_q_heads(ref, blk, c):
    lane = lax.broadcasted_iota(I32, blk.shape, 1)
    low = lane < HEAD_DIM
    a, b = _pair_lower_upper(blk)
    ref[0, 2 * c] = jnp.where(low, a, 0.0).astype(ref.dtype)
    ref[0, 2 * c + 1] = jnp.where(low, b, 0.0).astype(ref.dtype)


def _store_kv_heads(ref, blk, c):
    a, b = _pair_lower_upper(blk)
    ref[0, 2 * c] = a.astype(ref.dtype)
    ref[0, 2 * c + 1] = b.astype(ref.dtype)


def _proj_a_body(x_ref, g_ref, w_ref, bf_ref, q_hm, k_out, k_hm, v_out, v_hm, sz_out, logf_out):
    h = _rms_bf16(x_ref, g_ref)
    q = _dot(h, w_ref[:, 0:512]) * (HEAD_DIM ** -0.5)
    for c in range(4):
        _store_q_heads(q_hm, q[:, c * LANES:(c + 1) * LANES], c)
    z = _dot(h, w_ref[:, 512:1024])
    sz_out[...] = (z * jax.nn.sigmoid(z)).astype(BF16)
    k = _dot(h, w_ref[:, 1024:1280])
    k_out[...] = k
    v = _dot(h, w_ref[:, 1280:1536])
    v_out[...] = v
    for c in range(2):
        _store_kv_heads(k_hm, k[:, c * LANES:(c + 1) * LANES], c)
        _store_kv_heads(v_hm, v[:, c * LANES:(c + 1) * LANES], c)
    f = _dot(h, w_ref[:, 1536:1664]) + bf_ref[...]
    logf_out[...] = (-_softplus(-f))[:, :N_HEADS]


def _proj_b_body(x_ref, g_ref, w_ref, cos_ref, sin_ref,
                 q_hm, iq_hm, sz_out, k_out, k_hm, v_out, v_hm, ik_out, ik_bf, iw_out):
    h = _rms_bf16(x_ref, g_ref)
    cos = cos_ref[...]
    sin = sin_ref[...]
    q = _dot(h, w_ref[:, 0:512])
    iq = _dot(h, w_ref[:, 512:1024])
    for c in range(4):
        sl = slice(c * LANES, (c + 1) * LANES)
        _store_q_heads(q_hm, _rope128(q[:, sl], cos, sin) * (HEAD_DIM ** -0.5), c)
        _store_q_heads(iq_hm, _rope128(iq[:, sl], cos, sin) * (IDX_DIM ** -0.5), c)
    z = _dot(h, w_ref[:, 1024:1536])
    sz_out[...] = (z * jax.nn.sigmoid(z)).astype(BF16)
    k = _dot(h, w_ref[:, 1536:1792])
    v = _dot(h, w_ref[:, 1792:2048])
    v_out[...] = v
    for c in range(2):
        sl = slice(c * LANES, (c + 1) * LANES)
        kr = _rope128(k[:, sl], cos, sin)
        k_out[:, sl] = kr
        _store_kv_heads(k_hm, kr, c)
        _store_kv_heads(v_hm, v[:, sl], c)
    ik = _rope128(_dot(h, w_ref[:, 2048:2176]), cos, sin)
    ik_out[...] = ik[:, :IDX_DIM]
    ik_bf[...] = ik.astype(BF16)
    iw = _dot(h, w_ref[:, 2176:2304])
    iw_out[...] = (iw * (IDX_HEADS ** -0.5))[:, :IDX_HEADS]


def _proj_c_body(x_ref, g_ref, w_ref, q_hm, sz_out, k_out, k_hm, v_out, v_hm):
    h = _rms_bf16(x_ref, g_ref)
    q = _dot(h, w_ref[:, 0:512]) * (HEAD_DIM ** -0.5)
    for c in range(4):
        _store_q_heads(q_hm, q[:, c * LANES:(c + 1) * LANES], c)
    z = _dot(h, w_ref[:, 512:1024])
    sz_out[...] = (z * jax.nn.sigmoid(z)).astype(BF16)
    k = _dot(h, w_ref[:, 1024:1280])
    k_out[...] = k
    v = _dot(h, w_ref[:, 1280:1536])
    v_out[...] = v
    for c in range(2):
        _store_kv_heads(k_hm, k[:, c * LANES:(c + 1) * LANES], c)
        _store_kv_heads(v_hm, v[:, c * LANES:(c + 1) * LANES], c)


def _proj_g_body(x_ref, g_ref, w_ref, sg_out):
    h = _rms_bf16(x_ref, g_ref)
    for c in range(N_BRANCH):
        sl = slice(c * D_MODEL, (c + 1) * D_MODEL)
        sg_out[:, sl] = jax.nn.sigmoid(_dot(h, w_ref[:, sl])).astype(BF16)


def _project(x2d, bx, sx, tm, norm_g, wa, wb, wc, wg, bfp, cos_rows, sin_rows):
    m = bx * sx
    ns = sx // tm
    grid = (m // tm,)
    row = lambda w: pl.BlockSpec((tm, w), lambda i: (i, 0))
    full = lambda a: pl.BlockSpec(a.shape, lambda i: (0,) * a.ndim)
    hm = lambda nh: pl.BlockSpec((1, nh, tm, LANES), lambda i: (i // ns, 0, i % ns, 0))
    hm_shape = lambda nh: jax.ShapeDtypeStruct((bx, nh, sx, LANES), BF16)
    rows = lambda w, dt: jax.ShapeDtypeStruct((m, w), dt)
    tab = pl.BlockSpec((tm, LANES), lambda i: (i % ns, 0))
    g2 = norm_g.reshape(1, D_MODEL)

    a = pl.pallas_call(
        _proj_a_body, grid=grid, name="proj_a",
        in_specs=[row(D_MODEL), full(g2), full(wa), full(bfp)],
        out_specs=[hm(N_HEADS), row(KV_WIDTH), hm(KV_HEADS), row(KV_WIDTH), hm(KV_HEADS),
                   row(D_BRANCH), row(N_HEADS)],
        out_shape=[hm_shape(N_HEADS), rows(KV_WIDTH, F32), hm_shape(KV_HEADS), rows(KV_WIDTH, F32),
                   hm_shape(KV_HEADS), rows(D_BRANCH, BF16), rows(N_HEADS, F32)],
        compiler_params=_cparams("parallel"),
    )(x2d, g2, wa, bfp)
    b = pl.pallas_call(
        _proj_b_body, grid=grid, name="proj_b",
        in_specs=[row(D_MODEL), full(g2), full(wb), tab, tab],
        out_specs=[hm(N_HEADS), hm(IDX_HEADS), row(D_BRANCH), row(KV_WIDTH), hm(KV_HEADS),
                   row(KV_WIDTH), hm(KV_HEADS), row(IDX_DIM), row(LANES), row(IDX_HEADS)],
        out_shape=[hm_shape(N_HEADS), hm_shape(IDX_HEADS), rows(D_BRANCH, BF16), rows(KV_WIDTH, F32),
                   hm_shape(KV_HEADS), rows(KV_WIDTH, F32), hm_shape(KV_HEADS), rows(IDX_DIM, F32),
                   rows(LANES, BF16), rows(IDX_HEADS, F32)],
        compiler_params=_cparams("parallel"),
    )(x2d, g2, wb, cos_rows, sin_rows)
    c = pl.pallas_call(
        _proj_c_body, grid=grid, name="proj_c",
        in_specs=[row(D_MODEL), full(g2), full(wc)],
        out_specs=[hm(N_HEADS), row(D_BRANCH), row(KV_WIDTH), hm(KV_HEADS), row(KV_WIDTH), hm(KV_HEADS)],
        out_shape=[hm_shape(N_HEADS), rows(D_BRANCH, BF16), rows(KV_WIDTH, F32), hm_shape(KV_HEADS),
                   rows(KV_WIDTH, F32), hm_shape(KV_HEADS)],
        compiler_params=_cparams("parallel"),
    )(x2d, g2, wc)
    sg = pl.pallas_call(
        _proj_g_body, grid=grid, name="proj_g",
        in_specs=[row(D_MODEL), full(g2), full(wg)],
        out_specs=row(N_BRANCH * D_MODEL),
        out_shape=rows(N_BRANCH * D_MODEL, BF16),
        compiler_params=_cparams("parallel"),
    )(x2d, g2, wg)
    keys_a = ("q_hm", "k", "k_hm", "v", "v_hm", "sz", "logf")
    keys_b = ("q_hm", "iq_hm", "sz", "k", "k_hm", "v", "v_hm", "ik", "ik_bf", "iw")
    keys_c = ("q_hm", "sz", "k", "k_hm", "v", "v_hm")
    return dict(zip(keys_a, a)), dict(zip(keys_b, b)), dict(zip(keys_c, c)), sg


def _cumsum_body(x_ref, o_ref, *, n_chunks):
    r = lax.broadcasted_iota(I32, (CUMSUM_CHUNK, CUMSUM_CHUNK), 0)
    c = lax.broadcasted_iota(I32, (CUMSUM_CHUNK, CUMSUM_CHUNK), 1)
    tri = (r <= c).astype(F32)
    carry = jnp.zeros((N_HEADS, 1), F32)
    for ch in range(n_chunks):
        sl = slice(ch * CUMSUM_CHUNK, (ch + 1) * CUMSUM_CHUNK)
        inc = jnp.dot(x_ref[0, :, sl], tri, preferred_element_type=F32,
                      precision=lax.Precision.HIGHEST) + carry
        o_ref[0, :, sl] = inc
        carry = inc[:, CUMSUM_CHUNK - 1:CUMSUM_CHUNK]


def _cumsum_heads(logf_t):
    b, nh, s = logf_t.shape
    spec = pl.BlockSpec((1, nh, s), lambda i: (i, 0, 0))
    return pl.pallas_call(
        functools.partial(_cumsum_body, n_chunks=s // CUMSUM_CHUNK), grid=(b,), name="logf_cumsum",
        in_specs=[spec], out_specs=spec, out_shape=jax.ShapeDtypeStruct(logf_t.shape, F32),
        compiler_params=_cparams("parallel"),
    )(logf_t)


def _causal_pairs(nq, tq, tk, descending):
    it, jt = [], []
    for i in range(nq):
        jmax = ((i + 1) * tq - 1) // tk
        js = range(jmax, -1, -1) if descending else range(jmax + 1)
        for j in js:
            it.append(i)
            jt.append(j)
    return jnp.asarray(np.asarray(it, np.int32)), jnp.asarray(np.asarray(jt, np.int32))


def _last_kblock(i, tq, tk):
    return ((i + 1) * tq - 1) // tk


def _pair_lane_select(top, bot):
    lane = lax.broadcasted_iota(I32, top.shape, 1)
    return jnp.where(lane < HEAD_DIM, top, bot)


def _fox_body(it_ref, jt_ref, q_ref, k_ref, v_ref, dq_ref, dk_ref, o_ref, m_sc, l_sc, acc_sc, *, tq, tk):
    step = pl.program_id(2)
    i = it_ref[step]
    j = jt_ref[step]

    @pl.when(j == 0)
    def _():
        m_sc[...] = jnp.full(m_sc.shape, NEG_BIG, F32)
        l_sc[...] = jnp.zeros(l_sc.shape, F32)
        acc_sc[...] = jnp.zeros(acc_sc.shape, F32)

    q2 = q_ref[0].reshape(2 * tq, LANES)
    s = _dot_nt(q2, k_ref[0, 0])
    v = v_ref[0, 0]
    row = i * tq + lax.broadcasted_iota(I32, (tq, tk), 0)
    col = j * tk + lax.broadcasted_iota(I32, (tq, tk), 1)
    causal = col <= row
    for r in range(2):
        sl = slice(r * tq, (r + 1) * tq)
        logit = jnp.where(causal, s[sl] + (dq_ref[0, r] - dk_ref[0, r]), NEG_BIG)
        m_old = m_sc[sl]
        m_new = jnp.maximum(m_old, jnp.max(logit, axis=1, keepdims=True))
        alpha = jnp.exp(m_old - m_new)
        p = jnp.exp(logit - m_new)
        l_sc[sl] = alpha * l_sc[sl] + jnp.sum(p, axis=1, keepdims=True)
        acc_sc[sl] = alpha * acc_sc[sl] + _dot(p.astype(BF16), v)
        m_sc[sl] = m_new

    @pl.when(j == _last_kblock(i, tq, tk))
    def _():
        o = acc_sc[...] / l_sc[...]
        o_ref[0] = _pair_lane_select(o[:tq], o[tq:])


def _fox_attention(q_hm, k_hm, v_hm, dcum_t, tq, tk):
    b, _, s, _ = q_hm.shape
    it, jt = _causal_pairs(s // tq, tq, tk, descending=False)
    dq4 = dcum_t.reshape(b, N_HEADS, s, 1)
    dk4 = dcum_t.reshape(b, N_HEADS, 1, s)
    grid_spec = pltpu.PrefetchScalarGridSpec(
        num_scalar_prefetch=2, grid=(b, KV_HEADS, it.shape[0]),
        in_specs=[
            pl.BlockSpec((1, 2, tq, LANES), lambda bi, g, p, it, jt: (bi, g, it[p], 0)),
            pl.BlockSpec((1, 1, tk, LANES), lambda bi, g, p, it, jt: (bi, g, jt[p], 0)),
            pl.BlockSpec((1, 1, tk, LANES), lambda bi, g, p, it, jt: (bi, g, jt[p], 0)),
            pl.BlockSpec((1, 2, tq, 1), lambda bi, g, p, it, jt: (bi, g, it[p], 0)),
            pl.BlockSpec((1, 2, 1, tk), lambda bi, g, p, it, jt: (bi, g, 0, jt[p])),
        ],
        out_specs=pl.BlockSpec((1, tq, LANES), lambda bi, g, p, it, jt: (bi, it[p], g)),
        scratch_shapes=[pltpu.VMEM((2 * tq, 1), F32), pltpu.VMEM((2 * tq, 1), F32),
                        pltpu.VMEM((2 * tq, LANES), F32)],
    )
    return pl.pallas_call(
        functools.partial(_fox_body, tq=tq, tk=tk), grid_spec=grid_spec, name="fox_attn",
        out_shape=jax.ShapeDtypeStruct((b, s, D_BRANCH), F32),
        compiler_params=_cparams("parallel", "parallel", "arbitrary"),
    )(it, jt, q_hm, k_hm, v_hm, dq4, dk4)


def _strict_upper_tri(n):
    r = lax.broadcasted_iota(I32, (n, n), 0)
    c = lax.broadcasted_iota(I32, (n, n), 1)
    return (r > c).astype(BF16)


def _sb_body(it_ref, jt_ref, q_ref, k_ref, v_ref, o_ref, r_sc, acc_sc, *, tq, tk):
    step = pl.program_id(2)
    i = it_ref[step]
    j = jt_ref[step]

    @pl.when(j == _last_kblock(i, tq, tk))
    def _():
        r_sc[...] = jnp.zeros(r_sc.shape, F32)
        acc_sc[...] = jnp.zeros(acc_sc.shape, F32)

    q2 = q_ref[0].reshape(2 * tq, LANES)
    z = _dot_nt(q2, k_ref[0, 0])
    v = v_ref[0, 0]
    rr = lax.broadcasted_iota(I32, (2 * tq, tk), 0)
    row = i * tq + jnp.where(rr >= tq, rr - tq, rr)
    col = j * tk + lax.broadcasted_iota(I32, (2 * tq, tk), 1)
    valid = col < row
    sp = _softplus(z)
    log_keep = jnp.where(valid, -sp, 0.0)
    log_sig = z - sp
    sub = min(CUMSUM_CHUNK, tk)
    tri = _strict_upper_tri(sub)
    after = r_sc[...]
    acc = acc_sc[...]
    for c in reversed(range(tk // sub)):
        cs = slice(c * sub, (c + 1) * sub)
        lk = log_keep[:, cs]
        log_after = _dot_split(lk, tri) + after
        a = jnp.where(valid[:, cs], jnp.exp(log_sig[:, cs] + log_after), 0.0)
        acc = acc + _dot(a.astype(BF16), v[cs])
        after = after + jnp.sum(lk, axis=1, keepdims=True)
    r_sc[...] = after
    acc_sc[...] = acc

    @pl.when(j == 0)
    def _():
        o_ref[0] = _pair_lane_select(acc[:tq], acc[tq:])


def _sb_attention(q_hm, k_hm, v_hm, tq, tk):
    b, _, s, _ = q_hm.shape
    it, jt = _causal_pairs(s // tq, tq, tk, descending=True)
    grid_spec = pltpu.PrefetchScalarGridSpec(
        num_scalar_prefetch=2, grid=(b, KV_HEADS, it.shape[0]),
        in_specs=[
            pl.BlockSpec((1, 2, tq, LANES), lambda bi, g, p, it, jt: (bi, g, it[p], 0)),
            pl.BlockSpec((1, 1, tk, LANES), lambda bi, g, p, it, jt: (bi, g, jt[p], 0)),
            pl.BlockSpec((1, 1, tk, LANES), lambda bi, g, p, it, jt: (bi, g, jt[p], 0)),
        ],
        out_specs=pl.BlockSpec((1, tq, LANES), lambda bi, g, p, it, jt: (bi, it[p], g)),
        scratch_shapes=[pltpu.VMEM((2 * tq, 1), F32), pltpu.VMEM((2 * tq, LANES), F32)],
    )
    return pl.pallas_call(
        functools.partial(_sb_body, tq=tq, tk=tk), grid_spec=grid_spec, name="sb_attn",
        out_shape=jax.ShapeDtypeStruct((b, s, D_BRANCH), F32),
        compiler_params=_cparams("parallel", "parallel", "arbitrary"),
    )(it, jt, q_hm, k_hm, v_hm)


def _fold_lanes(x):
    part = x[:, 0:LANES]
    for g in range(1, x.shape[1] // LANES):
        part = part + x[:, g * LANES:(g + 1) * LANES]
    return part


def _dsa_index_body(iq_ref, iw_ref, ik_ref, keys_ref, thr_ref, cut_ref, *, tq, tkc, n_chunks, k_sel):
    i = pl.program_id(1)
    n_c = _last_kblock(i, tq, tkc) + 1
    w = iw_ref[0]
    row = i * tq + lax.broadcasted_iota(I32, (tq, tkc), 0)
    col0 = lax.broadcasted_iota(I32, (tq, tkc), 1)

    def score_chunk(c, carry):
        ikc = ik_ref[0, pl.ds(pl.multiple_of(c * tkc, tkc), tkc), :]
        acc = jnp.zeros((tq, tkc), F32)
        for h in range(IDX_HEADS):
            acc = acc + w[:, h:h + 1] * jnp.maximum(_dot_nt(iq_ref[0, h], ikc), 0.0)
        keys_ref[0, c] = jnp.where(c * tkc + col0 <= row, _sortable(acc), INT_MIN)
        return carry

    lax.fori_loop(0, n_c, score_chunk, 0)

    def fill_chunk(c, carry):
        keys_ref[0, c] = jnp.full((tq, tkc), INT_MIN, I32)
        return carry

    lax.fori_loop(n_c, n_chunks, fill_chunk, 0)

    def count(pred):
        def body(c, acc):
            return acc + _fold_lanes(jnp.where(pred(keys_ref[0, c], c), 1, 0).astype(I32))
        acc = lax.fori_loop(0, n_c, body, jnp.zeros((tq, LANES), I32))
        return jnp.sum(acc, axis=1, keepdims=True)

    t_pos = i * tq + lax.broadcasted_iota(I32, (tq, 1), 0)
    kk = jnp.minimum(t_pos + 1, k_sel)

    thr0 = jnp.where(count(lambda key, c: key >= 0) >= kk, 0, INT_MIN).astype(I32)

    def thr_step(b, thr):
        cand = thr + jnp.left_shift(jnp.int32(1), 30 - b)
        return jnp.where(count(lambda key, c: key >= cand) >= kk, cand, thr)

    thr = lax.fori_loop(0, 31, thr_step, thr0)
    need = kk - count(lambda key, c: key > thr)
    n_eq = count(lambda key, c: key == thr)
    thr_ref[0] = thr
    cut_ref[0] = jnp.full((tq, 1), n_chunks * tkc, I32)

    @pl.when(jnp.max(n_eq - need) > 0)
    def _():
        n_bits = (n_chunks * tkc - 1).bit_length()

        def cut_step(b, cut):
            cand = cut + jnp.left_shift(jnp.int32(1), n_bits - 1 - b)
            below = count(lambda key, c: (key == thr) & (c * tkc + col0 < cand))
            return jnp.where(below < need, cand, cut)

        cut_ref[0] = lax.fori_loop(0, n_bits, cut_step, jnp.zeros((tq, 1), I32))


def _dsa_index(iq_hm, iw, ik_bf, tq, tkc, k_sel):
    b, _, s, _ = iq_hm.shape
    n_chunks = s // tkc
    body = functools.partial(_dsa_index_body, tq=tq, tkc=tkc, n_chunks=n_chunks, k_sel=k_sel)
    return pl.pallas_call(
        body, grid=(b, s // tq), name="dsa_index",
        in_specs=[pl.BlockSpec((1, IDX_HEADS, tq, LANES), lambda bi, i: (bi, 0, i, 0)),
                  pl.BlockSpec((1, tq, IDX_HEADS), lambda bi, i: (bi, i, 0)),
                  pl.BlockSpec((1, s, LANES), lambda bi, i: (bi, 0, 0))],
        out_specs=[pl.BlockSpec((1, n_chunks, tq, tkc), lambda bi, i: (bi, 0, i, 0)),
                   pl.BlockSpec((1, tq, 1), lambda bi, i: (bi, i, 0)),
                   pl.BlockSpec((1, tq, 1), lambda bi, i: (bi, i, 0))],
        out_shape=[jax.ShapeDtypeStruct((b, n_chunks, s, tkc), I32),
                   jax.ShapeDtypeStruct((b, s, 1), I32), jax.ShapeDtypeStruct((b, s, 1), I32)],
        compiler_params=_cparams("parallel", "parallel"),
    )(iq_hm, iw, ik_bf)


def _selected(key, thr, cut, col):
    return (key > thr) | ((key == thr) & (col <= cut))


def _dsa_attn_body(it_ref, jt_ref, q_ref, k_ref, v_ref, keys_ref, thr_ref, cut_ref, o_ref,
                   m_sc, l_sc, acc_sc, *, tq, tk):
    step = pl.program_id(2)
    i = it_ref[step]
    j = jt_ref[step]

    @pl.when(j == 0)
    def _():
        m_sc[...] = jnp.full(m_sc.shape, NEG_BIG, F32)
        l_sc[...] = jnp.zeros(l_sc.shape, F32)
        acc_sc[...] = jnp.zeros(acc_sc.shape, F32)

    q2 = q_ref[0].reshape(2 * tq, LANES)
    s = _dot_nt(q2, k_ref[0, 0])
    v = v_ref[0, 0]
    col = j * tk + lax.broadcasted_iota(I32, (tq, tk), 1)
    sel = _selected(keys_ref[0, 0], thr_ref[0], cut_ref[0], col)
    for r in range(2):
        sl = slice(r * tq, (r + 1) * tq)
        logit = jnp.where(sel, s[sl], NEG_BIG)
        m_old = m_sc[sl]
        m_new = jnp.maximum(m_old, jnp.max(logit, axis=1, keepdims=True))
        alpha = jnp.exp(m_old - m_new)
        p = jnp.where(sel, jnp.exp(logit - m_new), 0.0)
        l_sc[sl] = alpha * l_sc[sl] + jnp.sum(p, axis=1, keepdims=True)
        acc_sc[sl] = alpha * acc_sc[sl] + _dot(p.astype(BF16), v)
        m_sc[sl] = m_new

    @pl.when(j == _last_kblock(i, tq, tk))
    def _():
        o = acc_sc[...] / l_sc[...]
        o_ref[0] = _pair_lane_select(o[:tq], o[tq:])


def _dsa_attention(q_hm, k_hm, v_hm, keys, thr, cut, tq, tk):
    b, _, s, _ = q_hm.shape
    it, jt = _causal_pairs(s // tq, tq, tk, descending=False)
    grid_spec = pltpu.PrefetchScalarGridSpec(
        num_scalar_prefetch=2, grid=(b, KV_HEADS, it.shape[0]),
        in_specs=[
            pl.BlockSpec((1, 2, tq, LANES), lambda bi, g, p, it, jt: (bi, g, it[p], 0)),
            pl.BlockSpec((1, 1, tk, LANES), lambda bi, g, p, it, jt: (bi, g, jt[p], 0)),
            pl.BlockSpec((1, 1, tk, LANES), lambda bi, g, p, it, jt: (bi, g, jt[p], 0)),
            pl.BlockSpec((1, 1, tq, tk), lambda bi, g, p, it, jt: (bi, jt[p], it[p], 0)),
            pl.BlockSpec((1, tq, 1), lambda bi, g, p, it, jt: (bi, it[p], 0)),
            pl.BlockSpec((1, tq, 1), lambda bi, g, p, it, jt: (bi, it[p], 0)),
        ],
        out_specs=pl.BlockSpec((1, tq, LANES), lambda bi, g, p, it, jt: (bi, it[p], g)),
        scratch_shapes=[pltpu.VMEM((2 * tq, 1), F32), pltpu.VMEM((2 * tq, 1), F32),
                        pltpu.VMEM((2 * tq, LANES), F32)],
    )
    return pl.pallas_call(
        functools.partial(_dsa_attn_body, tq=tq, tk=tk), grid_spec=grid_spec, name="dsa_attn",
        out_shape=jax.ShapeDtypeStruct((b, s, D_BRANCH), F32),
        compiler_params=_cparams("parallel", "parallel", "arbitrary"),
    )(it, jt, q_hm, k_hm, v_hm, keys, thr, cut)


def _merge_body(oa_ref, ob_ref, oc_ref, sza_ref, szb_ref, szc_ref, sg_ref, x_ref, wb_ref, wo_ref, fg_ref,
                out_ref, *, final_norm):
    m = jnp.zeros(x_ref.shape, F32)
    for n, (o_ref, sz_ref) in enumerate(((oa_ref, sza_ref), (ob_ref, szb_ref), (oc_ref, szc_ref))):
        u = (o_ref[...] * sz_ref[...].astype(F32)).astype(BF16)
        y = _dot(u, wb_ref[n])
        m = m + sg_ref[:, n * D_MODEL:(n + 1) * D_MODEL].astype(F32) * y
    r = x_ref[...] + _dot(m.astype(BF16), wo_ref[...])
    if final_norm:
        ms = jnp.mean(r * r, axis=-1, keepdims=True)
        r = r * lax.rsqrt(ms + RMS_EPS) * fg_ref[...]
    out_ref[...] = r


def _merge(oa, ob, oc, sza, szb, szc, sg, x2d, wbr, wo, final_g, tm, final_norm):
    m = x2d.shape[0]
    row = lambda w: pl.BlockSpec((tm, w), lambda i: (i, 0))
    full = lambda a: pl.BlockSpec(a.shape, lambda i: (0,) * a.ndim)
    fg = final_g.reshape(1, D_MODEL)
    return pl.pallas_call(
        functools.partial(_merge_body, final_norm=final_norm), grid=(m // tm,), name="merge",
        in_specs=[row(D_BRANCH)] * 6 + [row(N_BRANCH * D_MODEL), row(D_MODEL), full(wbr), full(wo), full(fg)],
        out_specs=row(D_MODEL), out_shape=jax.ShapeDtypeStruct((m, D_MODEL), F32),
        compiler_params=_cparams("parallel"),
    )(oa, ob, oc, sza, szb, szc, sg, x2d, wbr, wo, fg)


def _dec_index_body(pt_ref, iq_ref, iw_ref, ikn_ref, ikc_ref, sc_ref, scn_ref):
    p = pl.program_id(1)
    iq = iq_ref[0][:, :IDX_DIM]
    w = iw_ref[0]
    r = jnp.maximum(_dot_nt(iq, ikc_ref[0, 0].astype(BF16)), 0.0)
    sc_ref[0, 0] = jnp.sum(w * r, axis=0, keepdims=True)

    @pl.when(p == 0)
    def _():
        rn = jnp.maximum(jnp.sum(iq.astype(F32) * ikn_ref[0], axis=1, keepdims=True), 0.0)
        scn_ref[0] = jnp.broadcast_to(jnp.sum(w * rn, axis=0, keepdims=True), (1, LANES))


def _dec_index(page_table, layer, iq_d, iw_d, ik_new, cache_ik):
    db = iq_d.shape[0]
    n_pages = page_table.shape[0] // db
    page = cache_ik.shape[2]
    grid_spec = pltpu.PrefetchScalarGridSpec(
        num_scalar_prefetch=1, grid=(db, n_pages),
        in_specs=[
            pl.BlockSpec((1, IDX_HEADS, LANES), lambda b, p, pt: (b, 0, 0)),
            pl.BlockSpec((1, IDX_HEADS, 1), lambda b, p, pt: (b, 0, 0)),
            pl.BlockSpec((1, 1, IDX_DIM), lambda b, p, pt: (b, 0, 0)),
            pl.BlockSpec((1, 1, page, IDX_DIM), lambda b, p, pt: (layer, pt[b * n_pages + p], 0, 0)),
        ],
        out_specs=[pl.BlockSpec((1, 1, 1, page), lambda b, p, pt: (b, p, 0, 0)),
                   pl.BlockSpec((1, 1, LANES), lambda b, p, pt: (b, 0, 0))],
    )
    return pl.pallas_call(
        _dec_index_body, grid_spec=grid_spec, name="dec_index",
        out_shape=[jax.ShapeDtypeStruct((db, n_pages, 1, page), F32),
                   jax.ShapeDtypeStruct((db, 1, LANES), F32)],
        compiler_params=_cparams("parallel", "arbitrary"),
    )(page_table, iq_d, iw_d, ik_new, cache_ik)


def _dec_select_body(sc_ref, thr_ref, cut_ref, *, n_keys, k_sel):
    sc = sc_ref[...]
    col = lax.broadcasted_iota(I32, sc.shape, 1)
    keys = jnp.where(col < n_keys, _sortable(sc), INT_MIN)
    count = lambda m: jnp.sum(jnp.where(m, 1, 0).astype(I32), axis=1, keepdims=True)
    thr0 = jnp.where(count(keys >= 0) >= k_sel, 0, INT_MIN).astype(I32)

    def thr_step(b, thr):
        cand = thr + jnp.left_shift(jnp.int32(1), 30 - b)
        return jnp.where(count(keys >= cand) >= k_sel, cand, thr)

    thr = lax.fori_loop(0, 31, thr_step, thr0)
    need = k_sel - count(keys > thr)
    n_bits = (sc.shape[1] - 1).bit_length()

    def cut_step(b, cut):
        cand = cut + jnp.left_shift(jnp.int32(1), n_bits - 1 - b)
        return jnp.where(count((keys == thr) & (col < cand)) < need, cand, cut)

    thr_ref[...] = thr
    cut_ref[...] = lax.fori_loop(0, n_bits, cut_step, jnp.zeros(thr.shape, I32))


def _dec_select(scores, n_keys, k_sel):
    db = scores.shape[0]
    full = lambda shape: pl.BlockSpec(shape, lambda i: (0,) * len(shape))
    return pl.pallas_call(
        functools.partial(_dec_select_body, n_keys=n_keys, k_sel=k_sel), grid=(1,), name="dec_select",
        in_specs=[full(scores.shape)], out_specs=[full((db, 1)), full((db, 1))],
        out_shape=[jax.ShapeDtypeStruct((db, 1), I32), jax.ShapeDtypeStruct((db, 1), I32)],
        compiler_params=_cparams("arbitrary"),
    )(scores)


def _dec_attn_body(pt_ref, qa_ref, qb_ref, qc_ref,
                   ak_ref, av_ref, lf_ref, bk_ref, bv_ref, ck_ref, cv_ref,
                   bsc_ref, bscn_ref, thr_ref, cut_ref,
                   akn_ref, avn_ref, lfn_ref, bkn_ref, bvn_ref,
                   oa_ref, ob_ref, oc_ref,
                   am_sc, al_sc, aacc_sc, ad_sc, bm_sc, bl_sc, bacc_sc, cr_sc, cacc_sc, *, n_pages, page):
    p = pl.program_id(1)
    pg = n_pages - 1 - p
    qa = qa_ref[0]
    qb = qb_ref[0]
    qc = qc_ref[0]
    thr = thr_ref[0]
    cut = cut_ref[0]
    tri = _strict_upper_tri(page)
    kvw = KV_WIDTH

    @pl.when(p == 0)
    def _():
        am_sc[...] = jnp.sum(qa.astype(F32) * akn_ref[0], axis=1, keepdims=True)
        al_sc[...] = jnp.ones(al_sc.shape, F32)
        aacc_sc[...] = jnp.broadcast_to(avn_ref[0], (N_HEADS, kvw))
        ad_sc[...] = lfn_ref[0]
        sn = jnp.sum(qb.astype(F32) * bkn_ref[0], axis=1, keepdims=True)
        keyn = _sortable(bscn_ref[0][:, 0:1])
        seln = _selected(keyn, thr, cut, jnp.int32(n_pages * page))
        bm_sc[...] = jnp.where(seln, sn, NEG_BIG)
        bl_sc[...] = jnp.where(seln, 1.0, 0.0) * jnp.ones(bl_sc.shape, F32)
        bacc_sc[...] = jnp.where(seln, jnp.broadcast_to(bvn_ref[0], (N_HEADS, kvw)), 0.0)
        cr_sc[...] = jnp.zeros(cr_sc.shape, F32)
        cacc_sc[...] = jnp.zeros(cacc_sc.shape, F32)

    lf = lf_ref[0, 0]
    logit = _dot_nt(qa, ak_ref[0, 0].astype(BF16)) + _dot_split(lf, tri) + ad_sc[...]
    m_old = am_sc[...]
    m_new = jnp.maximum(m_old, jnp.max(logit, axis=1, keepdims=True))
    alpha = jnp.exp(m_old - m_new)
    pa = jnp.exp(logit - m_new)
    al_sc[...] = alpha * al_sc[...] + jnp.sum(pa, axis=1, keepdims=True)
    aacc_sc[...] = alpha * aacc_sc[...] + _dot(pa.astype(BF16), av_ref[0, 0].astype(BF16))
    am_sc[...] = m_new
    ad_sc[...] = ad_sc[...] + jnp.sum(lf, axis=1, keepdims=True)

    col = pg * page + lax.broadcasted_iota(I32, (1, page), 1)
    sel = _selected(_sortable(bsc_ref[0, 0]), thr, cut, col)
    logit = jnp.where(sel, _dot_nt(qb, bk_ref[0, 0].astype(BF16)), NEG_BIG)
    m_old = bm_sc[...]
    m_new = jnp.maximum(m_old, jnp.max(logit, axis=1, keepdims=True))
    alpha = jnp.exp(m_old - m_new)
    pb = jnp.where(sel, jnp.exp(logit - m_new), 0.0)
    bl_sc[...] = alpha * bl_sc[...] + jnp.sum(pb, axis=1, keepdims=True)
    bacc_sc[...] = alpha * bacc_sc[...] + _dot(pb.astype(BF16), bv_ref[0, 0].astype(BF16))
    bm_sc[...] = m_new

    z = _dot_nt(qc, ck_ref[0, 0].astype(BF16))
    sp = _softplus(z)
    log_after = _dot_split(-sp, tri) + cr_sc[...]
    a = jnp.exp(z - sp + log_after)
    cacc_sc[...] = cacc_sc[...] + _dot(a.astype(BF16), cv_ref[0, 0].astype(BF16))
    cr_sc[...] = cr_sc[...] - jnp.sum(sp, axis=1, keepdims=True)

    @pl.when(p == n_pages - 1)
    def _():
        hrow = lax.broadcasted_iota(I32, (N_HEADS, kvw), 0)
        lane = lax.broadcasted_iota(I32, (N_HEADS, kvw), 1)
        own = jnp.right_shift(lane, 6) == jnp.right_shift(hrow, 1)

        def own_head(acc):
            x = jnp.where(own, acc, 0.0)
            x = x[:, :LANES] + x[:, LANES:]
            return (x + pltpu.roll(x, HEAD_DIM, 1))[:, :HEAD_DIM]

        oa_ref[0] = own_head(aacc_sc[...] / al_sc[...])
        ob_ref[0] = own_head(bacc_sc[...] / bl_sc[...])
        oc_ref[0] = own_head(cacc_sc[...])


def _dec_attention(page_table, layer, qa, qb, qc, caches, logf_t, bsc, bscn, thr, cut, new_rows):
    db = qa.shape[0]
    n_pages = page_table.shape[0] // db
    ca_k, ca_v, cb_k, cb_v, cc_k, cc_v = caches
    page = ca_k.shape[2]
    akn, avn, lfn, bkn, bvn = new_rows
    per_seq = lambda shape: pl.BlockSpec((1,) + shape, lambda b, p, pt: (b,) + (0,) * len(shape))
    paged = lambda w: pl.BlockSpec((1, 1, page, w), lambda b, p, pt: (layer, pt[b * n_pages + n_pages - 1 - p], 0, 0))
    grid_spec = pltpu.PrefetchScalarGridSpec(
        num_scalar_prefetch=1, grid=(db, n_pages),
        in_specs=[
            per_seq((N_HEADS, KV_WIDTH)), per_seq((N_HEADS, KV_WIDTH)), per_seq((N_HEADS, KV_WIDTH)),
            paged(KV_WIDTH), paged(KV_WIDTH),
            pl.BlockSpec((1, 1, N_HEADS, page), lambda b, p, pt: (layer, pt[b * n_pages + n_pages - 1 - p], 0, 0)),
            paged(KV_WIDTH), paged(KV_WIDTH), paged(KV_WIDTH), paged(KV_WIDTH),
            pl.BlockSpec((1, 1, 1, page), lambda b, p, pt: (b, n_pages - 1 - p, 0, 0)),
            per_seq((1, LANES)), per_seq((1, 1)), per_seq((1, 1)),
            per_seq((1, KV_WIDTH)), per_seq((1, KV_WIDTH)), per_seq((N_HEADS, 1)),
            per_seq((1, KV_WIDTH)), per_seq((1, KV_WIDTH)),
        ],
        out_specs=[per_seq((N_HEADS, HEAD_DIM))] * 3,
        scratch_shapes=[pltpu.VMEM((N_HEADS, 1), F32), pltpu.VMEM((N_HEADS, 1), F32),
                        pltpu.VMEM((N_HEADS, KV_WIDTH), F32), pltpu.VMEM((N_HEADS, 1), F32),
                        pltpu.VMEM((N_HEADS, 1), F32), pltpu.VMEM((N_HEADS, 1), F32),
                        pltpu.VMEM((N_HEADS, KV_WIDTH), F32),
                        pltpu.VMEM((N_HEADS, 1), F32), pltpu.VMEM((N_HEADS, KV_WIDTH), F32)],
    )
    out = jax.ShapeDtypeStruct((db, N_HEADS, HEAD_DIM), F32)
    return pl.pallas_call(
        functools.partial(_dec_attn_body, n_pages=n_pages, page=page), grid_spec=grid_spec, name="dec_attn",
        out_shape=[out, out, out],
        compiler_params=_cparams("parallel", "arbitrary"),
    )(page_table, qa, qb, qc, ca_k, ca_v, logf_t, cb_k, cb_v, cc_k, cc_v,
      bsc, bscn, thr, cut, akn, avn, lfn, bkn, bvn)


def _rope_tables(pos):
    half = HEAD_DIM // 2
    inv_freq = ROPE_THETA ** (-jnp.arange(half, dtype=F32) / half)
    ang = pos.astype(F32)[:, None] * inv_freq[None, :]
    c = jnp.cos(ang)
    s = jnp.sin(ang)
    return jnp.tile(c, (1, 4)), jnp.tile(jnp.concatenate([-s, s], axis=1), (1, 2))


def _pad_cols(w, width):
    return jnp.pad(w, ((0, 0), (0, 0), (0, width - w.shape[-1])))


def _group_weights(w_in):
    splits = np.cumsum(PROJ_SIZES)[:-1].tolist()
    (aq, ak, av, az, af, bq, bk, bv, bz, biq, bik, biw, cq, ck, cv, cz, gates) = jnp.split(w_in, splits, axis=-1)
    wa = jnp.concatenate([aq, az, ak, av, _pad_cols(af, LANES)], axis=-1)
    wb = jnp.concatenate([bq, biq, bz, bk, bv, _pad_cols(bik, LANES), _pad_cols(biw, LANES)], axis=-1)
    wc = jnp.concatenate([cq, cz, ck, cv], axis=-1)
    return wa.astype(BF16), wb.astype(BF16), wc.astype(BF16), gates.astype(BF16)


def _block_diag_q(q_hm):
    q = jnp.transpose(q_hm[0, :, :, :HEAD_DIM], (1, 0, 2))
    own = (jnp.arange(N_HEADS)[:, None] // (N_HEADS // KV_HEADS)) == jnp.arange(KV_HEADS)[None, :]
    qbd = jnp.where(own[None, :, :, None], q[:, :, None, :], jnp.zeros((), q.dtype))
    return qbd.reshape(q.shape[0], N_HEADS, KV_WIDTH)


def _layer_prompt(x, lw, tabs, tiles, final_g, final_norm):
    b, s, _ = x.shape
    tm, tq, tk, tq_idx = tiles
    x2d = x.reshape(b * s, D_MODEL)
    pa, pb, pc, sg = _project(x2d, b, s, tm, lw["norm_g"], lw["wa"], lw["wb"], lw["wc"], lw["wg"],
                              lw["bf"], tabs[0], tabs[1])
    logf = pa["logf"].reshape(b, s, N_HEADS)
    dcum_t = _cumsum_heads(jnp.transpose(logf, (0, 2, 1)))
    oa = _fox_attention(pa["q_hm"], pa["k_hm"], pa["v_hm"], dcum_t, tq, tk)
    k_sel = max(1, min(TOPK_MAX, s // 4))
    keys, thr, cut = _dsa_index(pb["iq_hm"], pb["iw"].reshape(b, s, IDX_HEADS),
                                pb["ik_bf"].reshape(b, s, LANES), tq_idx, tk, k_sel)
    ob = _dsa_attention(pb["q_hm"], pb["k_hm"], pb["v_hm"], keys, thr, cut, tq, tk)
    oc = _sb_attention(pc["q_hm"], pc["k_hm"], pc["v_hm"], tq, tk)
    out = _merge(oa.reshape(b * s, D_BRANCH), ob.reshape(b * s, D_BRANCH), oc.reshape(b * s, D_BRANCH),
                 pa["sz"], pb["sz"], pc["sz"], sg, x2d, lw["w_branch"], lw["w_out"], final_g, tm, final_norm)
    kv = lambda a: a.reshape(b, s, KV_HEADS, HEAD_DIM)
    new = (kv(pa["k"]), kv(pa["v"]), logf, kv(pb["k"]), kv(pb["v"]), pb["ik"].reshape(b, s, IDX_DIM),
           kv(pc["k"]), kv(pc["v"]))
    return out.reshape(b, s, D_MODEL), new


def _layer_sample(x, layer, page_table, caches, logf_t, cache_ik, lw, tabs, final_g, final_norm):
    db, t, _ = x.shape
    n_pages = page_table.shape[0] // db
    page = caches[0].shape[2]
    past = n_pages * page
    x2d = x.reshape(db * t, D_MODEL)
    pa, pb, pc, sg = _project(x2d, 1, db * t, db * t, lw["norm_g"], lw["wa"], lw["wb"], lw["wc"], lw["wg"],
                              lw["bf"], tabs[0], tabs[1])
    iq_d = jnp.transpose(pb["iq_hm"][0], (1, 0, 2))
    sc, scn = _dec_index(page_table, layer, iq_d, pb["iw"].reshape(db, IDX_HEADS, 1),
                         pb["ik"].reshape(db, 1, IDX_DIM), cache_ik)
    scores = jnp.concatenate([sc.reshape(db, past), scn[:, 0, :]], axis=1)
    k_sel = max(1, min(TOPK_MAX, (past + t) // 4))
    thr, cut = _dec_select(scores, past + t, k_sel)
    new_rows = (pa["k"].reshape(db, 1, KV_WIDTH), pa["v"].reshape(db, 1, KV_WIDTH),
                pa["logf"].reshape(db, N_HEADS, 1),
                pb["k"].reshape(db, 1, KV_WIDTH), pb["v"].reshape(db, 1, KV_WIDTH))
    oa, ob, oc = _dec_attention(page_table, layer, _block_diag_q(pa["q_hm"]), _block_diag_q(pb["q_hm"]),
                                _block_diag_q(pc["q_hm"]), caches, logf_t, sc, scn,
                                thr.reshape(db, 1, 1), cut.reshape(db, 1, 1), new_rows)
    out = _merge(oa.reshape(db, D_BRANCH), ob.reshape(db, D_BRANCH), oc.reshape(db, D_BRANCH),
                 pa["sz"], pb["sz"], pc["sz"], sg, x2d, lw["w_branch"], lw["w_out"], final_g, db * t, final_norm)
    kv = lambda a: a.reshape(db, t, KV_HEADS, HEAD_DIM)
    new = (kv(pa["k"]), kv(pa["v"]), pa["logf"].reshape(db, t, N_HEADS), kv(pb["k"]), kv(pb["v"]),
           pb["ik"].reshape(db, t, IDX_DIM), kv(pc["k"]), kv(pc["v"]))
    return out.reshape(db, t, D_MODEL), new


def _prompt_tiles(s):
    tq = min(512, s)
    return min(512, s), tq, tq, min(256, s)


def kernel(x_prompt, x_sample, cache_a_k, cache_a_v, cache_a_logf, cache_b_k, cache_b_v, cache_b_idx_k,
           cache_c_k, cache_c_v, page_table, norm_g, w_in, b_forget, w_branch, w_out, final_g):
    depth = w_in.shape[0]
    b, s, _ = x_prompt.shape
    db, t, _ = x_sample.shape
    assert t == 1, "decode path handles one new token per sequence"
    n_pool, page = cache_a_k.shape[1], cache_a_k.shape[2]
    past = page_table.shape[1] * page

    wa, wb, wc, wg = _group_weights(w_in)
    bfp = jnp.pad(b_forget, ((0, 0), (0, LANES - N_HEADS))).reshape(depth, 1, LANES)
    wbr = w_branch.astype(BF16)
    wo = w_out.astype(BF16)
    tabs_p = _rope_tables(jnp.arange(s))
    tabs_s = _rope_tables(jnp.full((db,), past))
    paged = lambda c: c.reshape(depth, n_pool, page, KV_WIDTH)
    caches = tuple(paged(c) for c in (cache_a_k, cache_a_v, cache_b_k, cache_b_v, cache_c_k, cache_c_v))
    logf_t = jnp.transpose(cache_a_logf, (0, 1, 3, 2))
    cache_ik = cache_b_idx_k
    pt_flat = page_table.reshape(-1)

    xp, xs = x_prompt, x_sample
    p_rows, s_rows = [], []
    for l in range(depth):
        lw = dict(norm_g=norm_g[l], wa=wa[l], wb=wb[l], wc=wc[l], wg=wg[l], bf=bfp[l],
                  w_branch=wbr[l], w_out=wo[l])
        last = l == depth - 1
        xp, pn = _layer_prompt(xp, lw, tabs_p, _prompt_tiles(s), final_g, last)
        xs, sn = _layer_sample(xs, l, pt_flat, caches, logf_t, cache_ik, lw, tabs_s, final_g, last)
        p_rows.append(pn)
        s_rows.append(sn)
    p_out = [jnp.stack([r[i] for r in p_rows]) for i in range(8)]
    s_out = [jnp.stack([r[i] for r in s_rows]) for i in range(8)]
    return (xp, xs, *p_out, *s_out)
```

```python
import functools

import numpy as np
import jax
import jax.numpy as jnp
from jax import lax
from jax.experimental import pallas as pl
from jax.experimental.pallas import tpu as pltpu

F32 = jnp.float32
BF16 = jnp.bfloat16
I32 = jnp.int32

D_MODEL = 1024
HEAD_DIM = 64
N_HEADS = 8
KV_HEADS = 4
D_BRANCH = N_HEADS * HEAD_DIM
KV_WIDTH = KV_HEADS * HEAD_DIM
N_BRANCH = 3
IDX_HEADS = 8
IDX_DIM = 64
TOPK_MAX = 256
ROPE_THETA = 10000.0
RMS_EPS = 1e-6

PROJ_SIZES = (
    D_BRANCH, KV_WIDTH, KV_WIDTH, D_BRANCH, N_HEADS,
    D_BRANCH, KV_WIDTH, KV_WIDTH, D_BRANCH, IDX_HEADS * IDX_DIM, IDX_DIM, IDX_HEADS,
    D_BRANCH, KV_WIDTH, KV_WIDTH, D_BRANCH,
    N_BRANCH * D_MODEL,
)

LANES = 128
SUBLANES = 8
CUMSUM_CHUNK = 256
VMEM_LIMIT_BYTES = 56 * 1024 * 1024
NEG_BIG = -1e30
INT_MIN = -(2 ** 31)
LOG2E = 1.4426950408889634
QK_SCALE2 = HEAD_DIM ** -0.5 * LOG2E


def _cparams(*sem):
    return pltpu.CompilerParams(dimension_semantics=sem, vmem_limit_bytes=VMEM_LIMIT_BYTES)


def _softplus(x):
    return jnp.maximum(x, 0.0) + jnp.log(1.0 + jnp.exp(-jnp.abs(x)))


def _dot(a, b):
    return jnp.dot(a, b, preferred_element_type=F32)


def _dot_nt(a, b):
    return lax.dot_general(a, b, (((1,), (1,)), ((), ())), preferred_element_type=F32)


def _dot_split(x, tri):
    hi = x.astype(BF16)
    lo = (x - hi.astype(F32)).astype(BF16)
    return _dot(hi, tri) + _dot(lo, tri)


def _sortable(x):
    x = jnp.where(x == 0.0, 0.0, x)
    bits = pltpu.bitcast(x, I32)
    return jnp.where(bits >= 0, bits, bits ^ 0x7FFFFFFF)


def _selected(key, thr, cut, pos):
    return (key > thr) | ((key == thr) & (pos <= cut))


def _strict_upper_tri(n):
    r = lax.broadcasted_iota(I32, (n, n), 0)
    c = lax.broadcasted_iota(I32, (n, n), 1)
    return (r > c).astype(BF16)


def _rms_bf16(x_ref, g_ref):
    xf = x_ref[...]
    ms = jnp.mean(xf * xf, axis=-1, keepdims=True)
    return (xf * lax.rsqrt(ms + RMS_EPS) * g_ref[...]).astype(BF16)


def _rope128(x, cos, sin):
    lane = lax.broadcasted_iota(I32, x.shape, 1)
    partner = jnp.where((lane & 32) == 0, pltpu.roll(x, 96, 1), pltpu.roll(x, 32, 1))
    return x * cos + partner * sin


def _pair_lower_upper(blk):
    lane = lax.broadcasted_iota(I32, blk.shape, 1)
    swapped = pltpu.roll(blk, 64, 1)
    low = lane < HEAD_DIM
    return jnp.where(low, blk, swapped), jnp.where(low, swapped, blk)


def _store_q_rows(ref, blk, c):
    lane = lax.broadcasted_iota(I32, blk.shape, 1)
    low = lane < HEAD_DIM
    a, b = _pair_lower_upper(blk)
    ref[0, 2 * c] = jnp.where(low, a, 0.0).astype(ref.dtype)
    ref[0, 2 * c + 1] = jnp.where(low, b, 0.0).astype(ref.dtype)


def _store_kv_rows(ref, blk, c):
    a, b = _pair_lower_upper(blk)
    ref[0, 2 * c] = a.astype(ref.dtype)
    ref[0, 2 * c + 1] = b.astype(ref.dtype)


def _store_q_cols(ref, blk, c):
    t = blk.T
    row = lax.broadcasted_iota(I32, t.shape, 0)
    ref[0, 2 * c] = jnp.where(row < HEAD_DIM, t, 0.0).astype(ref.dtype)
    ref[0, 2 * c + 1] = jnp.where(row < HEAD_DIM, 0.0, t).astype(ref.dtype)


def _store_v_cols(ref, blk, c):
    t = blk.T
    ts = pltpu.roll(blk, 64, 1).T
    row = lax.broadcasted_iota(I32, t.shape, 0)
    ref[0, 2 * c] = jnp.where(row < HEAD_DIM, t, 1.0).astype(ref.dtype)
    ref[0, 2 * c + 1] = jnp.where(row < HEAD_DIM, ts, 1.0).astype(ref.dtype)


def _proj_a_body(x_ref, g_ref, w_ref, bf_ref, qt_out, k_out, k_hm, v_out, vt_out, sz_out, logf_out):
    h = _rms_bf16(x_ref, g_ref)
    q = _dot(h, w_ref[:, 0:512]) * QK_SCALE2
    for c in range(4):
        _store_q_cols(qt_out, q[:, c * LANES:(c + 1) * LANES], c)
    z = _dot(h, w_ref[:, 512:1024])
    sz_out[...] = (z * jax.nn.sigmoid(z)).astype(BF16)
    k = _dot(h, w_ref[:, 1024:1280])
    k_out[...] = k
    v = _dot(h, w_ref[:, 1280:1536])
    v_out[...] = v
    for c in range(2):
        _store_kv_rows(k_hm, k[:, c * LANES:(c + 1) * LANES], c)
        _store_v_cols(vt_out, v[:, c * LANES:(c + 1) * LANES], c)
    f = _dot(h, w_ref[:, 1536:1664]) + bf_ref[...]
    logf_out[...] = (-_softplus(-f))[:, :N_HEADS]


def _proj_b_body(x_ref, g_ref, w_ref, cos_ref, sin_ref,
                 qt_out, iqt_out, sz_out, k_out, k_hm, v_out, vt_out, ik_out, ik_bf, iwt_out):
    h = _rms_bf16(x_ref, g_ref)
    cos = cos_ref[...]
    sin = sin_ref[...]
    q = _dot(h, w_ref[:, 0:512])
    iq = _dot(h, w_ref[:, 512:1024])
    for c in range(4):
        sl = slice(c * LANES, (c + 1) * LANES)
        _store_q_cols(qt_out, _rope128(q[:, sl], cos, sin) * QK_SCALE2, c)
        _store_q_cols(iqt_out, _rope128(iq[:, sl], cos, sin) * (IDX_DIM ** -0.5), c)
    z = _dot(h, w_ref[:, 1024:1536])
    sz_out[...] = (z * jax.nn.sigmoid(z)).astype(BF16)
    k = _dot(h, w_ref[:, 1536:1792])
    v = _dot(h, w_ref[:, 1792:2048])
    v_out[...] = v
    for c in range(2):
        sl = slice(c * LANES, (c + 1) * LANES)
        kr = _rope128(k[:, sl], cos, sin)
        k_out[:, sl] = kr
        _store_kv_rows(k_hm, kr, c)
        _store_v_cols(vt_out, v[:, sl], c)
    ik = _rope128(_dot(h, w_ref[:, 2048:2176]), cos, sin)
    ik_out[...] = ik[:, :IDX_DIM]
    ik_bf[...] = (ik + pltpu.roll(ik, 64, 1)).astype(BF16)
    iw = _dot(h, w_ref[:, 2176:2304]) * (IDX_HEADS ** -0.5)
    iwt_out[0] = iw.T[:IDX_HEADS]


def _proj_c_body(x_ref, g_ref, w_ref, q_hm, sz_out, k_out, k_hm, v_out, v_hm):
    h = _rms_bf16(x_ref, g_ref)
    q = _dot(h, w_ref[:, 0:512]) * (HEAD_DIM ** -0.5)
    for c in range(4):
        _store_q_rows(q_hm, q[:, c * LANES:(c + 1) * LANES], c)
    z = _dot(h, w_ref[:, 512:1024])
    sz_out[...] = (z * jax.nn.sigmoid(z)).astype(BF16)
    k = _dot(h, w_ref[:, 1024:1280])
    k_out[...] = k
    v = _dot(h, w_ref[:, 1280:1536])
    v_out[...] = v
    for c in range(2):
        _store_kv_rows(k_hm, k[:, c * LANES:(c + 1) * LANES], c)
        _store_kv_rows(v_hm, v[:, c * LANES:(c + 1) * LANES], c)


def _proj_g_body(x_ref, g_ref, w_ref, sg_out):
    h = _rms_bf16(x_ref, g_ref)
    for c in range(N_BRANCH):
        sl = slice(c * D_MODEL, (c + 1) * D_MODEL)
        sg_out[:, sl] = jax.nn.sigmoid(_dot(h, w_ref[:, sl])).astype(BF16)


def _project(x2d, bx, sx, tm, norm_g, wa, wb, wc, wg, bfp, cos_rows, sin_rows):
    m = bx * sx
    ns = sx // tm
    grid = (m // tm,)
    row = lambda w: pl.BlockSpec((tm, w), lambda i: (i, 0))
    full = lambda a: pl.BlockSpec(a.shape, lambda i: (0,) * a.ndim)
    hm = lambda nh: pl.BlockSpec((1, nh, tm, LANES), lambda i: (i // ns, 0, i % ns, 0))
    hm_shape = lambda nh: jax.ShapeDtypeStruct((bx, nh, sx, LANES), BF16)
    ht = lambda nh: pl.BlockSpec((1, nh, LANES, tm), lambda i: (i // ns, 0, 0, i % ns))
    ht_shape = lambda nh: jax.ShapeDtypeStruct((bx, nh, LANES, sx), BF16)
    rows = lambda w, dt: jax.ShapeDtypeStruct((m, w), dt)
    tab = pl.BlockSpec((tm, LANES), lambda i: (i % ns, 0))
    g2 = norm_g.reshape(1, D_MODEL)

    a = pl.pallas_call(
        _proj_a_body, grid=grid, name="proj_a",
        in_specs=[row(D_MODEL), full(g2), full(wa), full(bfp)],
        out_specs=[ht(N_HEADS), row(KV_WIDTH), hm(KV_HEADS), row(KV_WIDTH), ht(KV_HEADS),
                   row(D_BRANCH), row(N_HEADS)],
        out_shape=[ht_shape(N_HEADS), rows(KV_WIDTH, F32), hm_shape(KV_HEADS), rows(KV_WIDTH, F32),
                   ht_shape(KV_HEADS), rows(D_BRANCH, BF16), rows(N_HEADS, F32)],
        compiler_params=_cparams("parallel"),
    )(x2d, g2, wa, bfp)
    b = pl.pallas_call(
        _proj_b_body, grid=grid, name="proj_b",
        in_specs=[row(D_MODEL), full(g2), full(wb), tab, tab],
        out_specs=[ht(N_HEADS), ht(IDX_HEADS), row(D_BRANCH), row(KV_WIDTH), hm(KV_HEADS),
                   row(KV_WIDTH), ht(KV_HEADS), row(IDX_DIM), row(LANES),
                   pl.BlockSpec((1, IDX_HEADS, tm), lambda i: (i // ns, 0, i % ns))],
        out_shape=[ht_shape(N_HEADS), ht_shape(IDX_HEADS), rows(D_BRANCH, BF16), rows(KV_WIDTH, F32),
                   hm_shape(KV_HEADS), rows(KV_WIDTH, F32), ht_shape(KV_HEADS), rows(IDX_DIM, F32),
                   rows(LANES, BF16), jax.ShapeDtypeStruct((bx, IDX_HEADS, sx), F32)],
        compiler_params=_cparams("parallel"),
    )(x2d, g2, wb, cos_rows, sin_rows)
    c = pl.pallas_call(
        _proj_c_body, grid=grid, name="proj_c",
        in_specs=[row(D_MODEL), full(g2), full(wc)],
        out_specs=[hm(N_HEADS), row(D_BRANCH), row(KV_WIDTH), hm(KV_HEADS), row(KV_WIDTH), hm(KV_HEADS)],
        out_shape=[hm_shape(N_HEADS), rows(D_BRANCH, BF16), rows(KV_WIDTH, F32), hm_shape(KV_HEADS),
                   rows(KV_WIDTH, F32), hm_shape(KV_HEADS)],
        compiler_params=_cparams("parallel"),
    )(x2d, g2, wc)
    sg = pl.pallas_call(
        _proj_g_body, grid=grid, name="proj_g",
        in_specs=[row(D_MODEL), full(g2), full(wg)],
        out_specs=row(N_BRANCH * D_MODEL),
        out_shape=rows(N_BRANCH * D_MODEL, BF16),
        compiler_params=_cparams("parallel"),
    )(x2d, g2, wg)
    keys_a = ("qt", "k", "k_hm", "v", "vt", "sz", "logf")
    keys_b = ("qt", "iqt", "sz", "k", "k_hm", "v", "vt", "ik", "ik_bf", "iwt")
    keys_c = ("q_hm", "sz", "k", "k_hm", "v", "v_hm")
    return dict(zip(keys_a, a)), dict(zip(keys_b, b)), dict(zip(keys_c, c)), sg


def _cumsum_body(x_ref, o_ref, *, n_chunks):
    r = lax.broadcasted_iota(I32, (CUMSUM_CHUNK, CUMSUM_CHUNK), 0)
    c = lax.broadcasted_iota(I32, (CUMSUM_CHUNK, CUMSUM_CHUNK), 1)
    tri = (r <= c).astype(F32)
    carry = jnp.zeros((N_HEADS, 1), F32)
    for ch in range(n_chunks):
        sl = slice(ch * CUMSUM_CHUNK, (ch + 1) * CUMSUM_CHUNK)
        inc = jnp.dot(x_ref[0, :, sl], tri, preferred_element_type=F32,
                      precision=lax.Precision.HIGHEST) + carry
        o_ref[0, :, sl] = inc * LOG2E
        carry = inc[:, CUMSUM_CHUNK - 1:CUMSUM_CHUNK]


def _cumsum_heads(logf_t):
    b, nh, s = logf_t.shape
    spec = pl.BlockSpec((1, nh, s), lambda i: (i, 0, 0))
    return pl.pallas_call(
        functools.partial(_cumsum_body, n_chunks=s // CUMSUM_CHUNK), grid=(b,), name="logf_cumsum",
        in_specs=[spec], out_specs=spec, out_shape=jax.ShapeDtypeStruct(logf_t.shape, F32),
        compiler_params=_cparams("parallel"),
    )(logf_t)


def _causal_pairs(nq, tq, tk, descending):
    it, jt = [], []
    for i in range(nq):
        jmax = ((i + 1) * tq - 1) // tk
        js = range(jmax, -1, -1) if descending else range(jmax + 1)
        for j in js:
            it.append(i)
            jt.append(j)
    return jnp.asarray(np.asarray(it, np.int32)), jnp.asarray(np.asarray(jt, np.int32))


def _last_kblock(i, tq, tk):
    return ((i + 1) * tq - 1) // tk


def _finish_cols(acc_sc, o_ref):
    a0 = acc_sc[0]
    a1 = acc_sc[1]
    ot = jnp.concatenate([a0[:HEAD_DIM] / a0[HEAD_DIM:], a1[:HEAD_DIM] / a1[HEAD_DIM:]], axis=0)
    o_ref[0] = ot.T


def _softmax_cols_pair(logits, vt, m_sc, acc_sc):
    m_old = [m_sc[0], m_sc[1]]
    acc_old = [acc_sc[0], acc_sc[1]]
    for r in range(2):
        m_new = jnp.maximum(m_old[r], jnp.max(logits[r], axis=0, keepdims=True))
        p = jnp.exp2(logits[r] - m_new)
        acc_sc[r] = jnp.exp2(m_old[r] - m_new) * acc_old[r] + _dot(vt, p.astype(BF16))
        m_sc[r] = m_new


def _fox_body(it_ref, jt_ref, qt_ref, k_ref, vt_ref, dq_ref, dk_ref, o_ref, m_sc, acc_sc, *, tq, tk):
    step = pl.program_id(2)
    i = it_ref[step]
    j = jt_ref[step]

    @pl.when(j == 0)
    def _():
        m_sc[...] = jnp.full(m_sc.shape, NEG_BIG, F32)
        acc_sc[...] = jnp.zeros(acc_sc.shape, F32)

    def update(masked):
        k = k_ref[0, 0]
        logits = []
        for r in range(2):
            dk = jnp.tile(dk_ref[0, r], (1, tq // LANES))
            logits.append(_dot(k, qt_ref[0, r]) + (dq_ref[0, r] - dk))
        if masked:
            kpos = j * tk + lax.broadcasted_iota(I32, (tk, tq), 0)
            qpos = i * tq + lax.broadcasted_iota(I32, (tk, tq), 1)
            logits = [jnp.where(kpos <= qpos, lg, -jnp.inf) for lg in logits]
        _softmax_cols_pair(logits, vt_ref[0, 0], m_sc, acc_sc)

    all_visible = (j + 1) * tk - 1 <= i * tq

    @pl.when(all_visible)
    def _():
        update(False)

    @pl.when(jnp.logical_not(all_visible))
    def _():
        update(True)

    @pl.when(j == _last_kblock(i, tq, tk))
    def _():
        _finish_cols(acc_sc, o_ref)


def _fox_attention(qt, k_hm, vt, dcum_t, tq, tk):
    b, _, _, s = qt.shape
    it, jt = _causal_pairs(s // tq, tq, tk, descending=False)
    dq4 = dcum_t.reshape(b, N_HEADS, 1, s)
    dk4 = jnp.broadcast_to(dcum_t[..., None], (b, N_HEADS, s, LANES))
    grid_spec = pltpu.PrefetchScalarGridSpec(
        num_scalar_prefetch=2, grid=(b, KV_HEADS, it.shape[0]),
        in_specs=[
            pl.BlockSpec((1, 2, LANES, tq), lambda bi, g, p, it, jt: (bi, g, 0, it[p])),
            pl.BlockSpec((1, 1, tk, LANES), lambda bi, g, p, it, jt: (bi, g, jt[p], 0)),
            pl.BlockSpec((1, 1, LANES, tk), lambda bi, g, p, it, jt: (bi, g, 0, jt[p])),
            pl.BlockSpec((1, 2, 1, tq), lambda bi, g, p, it, jt: (bi, g, 0, it[p])),
            pl.BlockSpec((1, 2, tk, LANES), lambda bi, g, p, it, jt: (bi, g, jt[p], 0)),
        ],
        out_specs=pl.BlockSpec((1, tq, LANES), lambda bi, g, p, it, jt: (bi, it[p], g)),
        scratch_shapes=[pltpu.VMEM((2, 1, tq), F32), pltpu.VMEM((2, LANES, tq), F32)],
    )
    return pl.pallas_call(
        functools.partial(_fox_body, tq=tq, tk=tk), grid_spec=grid_spec, name="fox_attn",
        out_shape=jax.ShapeDtypeStruct((b, s, D_BRANCH), F32),
        compiler_params=_cparams("parallel", "parallel", "arbitrary"),
    )(it, jt, qt, k_hm, vt, dq4, dk4)


def _sb_body(it_ref, jt_ref, q_ref, k_ref, v_ref, o_ref, after_sc, acc_sc, *, tq, tk):
    step = pl.program_id(2)
    i = it_ref[step]
    j = jt_ref[step]

    @pl.when(j == _last_kblock(i, tq, tk))
    def _():
        after_sc[...] = jnp.zeros(after_sc.shape, F32)
        acc_sc[...] = jnp.zeros(acc_sc.shape, F32)

    sub = min(CUMSUM_CHUNK, tk)

    def update(masked):
        q2 = q_ref[0].reshape(2 * tq, LANES)
        z = _dot_nt(q2, k_ref[0, 0])
        v = v_ref[0, 0]
        sp = _softplus(z)
        log_sig = z - sp
        if masked:
            rr = lax.broadcasted_iota(I32, (2 * tq, tk), 0)
            row = i * tq + jnp.where(rr >= tq, rr - tq, rr)
            col = j * tk + lax.broadcasted_iota(I32, (2 * tq, tk), 1)
            valid = col < row
            sp = jnp.where(valid, sp, 0.0)
        r = lax.broadcasted_iota(I32, (sub, sub + LANES), 0)
        c = lax.broadcasted_iota(I32, (sub, sub + LANES), 1)
        neg_tri_ones = jnp.where((r > c) | (c >= sub), -1.0, 0.0).astype(BF16)
        after = after_sc[...]
        acc = acc_sc[...]
        for cidx in reversed(range(tk // sub)):
            cs = slice(cidx * sub, (cidx + 1) * sub)
            sums = _dot(sp[:, cs].astype(BF16), neg_tri_ones)
            a = jnp.exp(log_sig[:, cs] + sums[:, :sub] + jnp.tile(after, (1, sub // LANES)))
            if masked:
                a = jnp.where(valid[:, cs], a, 0.0)
            acc = acc + _dot(a.astype(BF16), v[cs])
            after = after + sums[:, sub:]
        after_sc[...] = after
        acc_sc[...] = acc

    all_valid = (j + 1) * tk - 1 < i * tq

    @pl.when(all_valid)
    def _():
        update(False)

    @pl.when(jnp.logical_not(all_valid))
    def _():
        update(True)

    @pl.when(j == 0)
    def _():
        acc = acc_sc[...]
        lane = lax.broadcasted_iota(I32, (tq, LANES), 1)
        o_ref[0] = jnp.where(lane < HEAD_DIM, acc[:tq], acc[tq:])


def _sb_attention(q_hm, k_hm, v_hm, tq, tk):
    b, _, s, _ = q_hm.shape
    it, jt = _causal_pairs(s // tq, tq, tk, descending=True)
    grid_spec = pltpu.PrefetchScalarGridSpec(
        num_scalar_prefetch=2, grid=(b, KV_HEADS, it.shape[0]),
        in_specs=[
            pl.BlockSpec((1, 2, tq, LANES), lambda bi, g, p, it, jt: (bi, g, it[p], 0)),
            pl.BlockSpec((1, 1, tk, LANES), lambda bi, g, p, it, jt: (bi, g, jt[p], 0)),
            pl.BlockSpec((1, 1, tk, LANES), lambda bi, g, p, it, jt: (bi, g, jt[p], 0)),
        ],
        out_specs=pl.BlockSpec((1, tq, LANES), lambda bi, g, p, it, jt: (bi, it[p], g)),
        scratch_shapes=[pltpu.VMEM((2 * tq, LANES), F32), pltpu.VMEM((2 * tq, LANES), F32)],
    )
    return pl.pallas_call(
        functools.partial(_sb_body, tq=tq, tk=tk), grid_spec=grid_spec, name="sb_attn",
        out_shape=jax.ShapeDtypeStruct((b, s, D_BRANCH), F32),
        compiler_params=_cparams("parallel", "parallel", "arbitrary"),
    )(it, jt, q_hm, k_hm, v_hm)


def _dsa_index_body(iqt_ref, iwt_ref, ik_ref, keys_ref, thr_ref, cut_ref, tie_ref, *, tq, tkc, n_chunks, k_sel):
    i = pl.program_id(1)
    n_c = _last_kblock(i, tq, tkc) + 1
    w = iwt_ref[0]
    kpos0 = lax.broadcasted_iota(I32, (tkc, tq), 0)
    qpos = i * tq + lax.broadcasted_iota(I32, (tkc, tq), 1)

    def chunk(c):
        return pl.ds(pl.multiple_of(c * tkc, tkc), tkc)

    def score_chunk(c, carry):
        ikc = ik_ref[0, chunk(c), :]
        acc = jnp.zeros((tkc, tq), F32)
        for h in range(IDX_HEADS):
            acc = acc + w[h:h + 1, :] * jnp.maximum(_dot(ikc, iqt_ref[0, h]), 0.0)
        keys_ref[0, 0, chunk(c), :] = jnp.where(c * tkc + kpos0 <= qpos, _sortable(acc), INT_MIN)
        return carry

    lax.fori_loop(0, n_c, score_chunk, 0)

    def fill_chunk(c, carry):
        keys_ref[0, 0, chunk(c), :] = jnp.full((tkc, tq), INT_MIN, I32)
        return carry

    lax.fori_loop(n_c, n_chunks, fill_chunk, 0)

    def count(pred):
        def body(c, acc):
            hit = jnp.where(pred(keys_ref[0, 0, chunk(c), :], c), 1, 0).astype(I32)
            return acc + jnp.sum(hit.reshape(tkc // SUBLANES, SUBLANES, tq), axis=0)
        acc = lax.fori_loop(0, n_c, body, jnp.zeros((SUBLANES, tq), I32))
        return jnp.sum(acc, axis=0, keepdims=True)

    kk = jnp.minimum(qpos[0:1, :] + 1, k_sel)

    thr0 = jnp.where(count(lambda key, c: key >= 0) >= kk, 0, INT_MIN).astype(I32)

    def thr_step(b, thr):
        cand = thr + jnp.left_shift(jnp.int32(1), 30 - b)
        return jnp.where(count(lambda key, c: key >= cand) >= kk, cand, thr)

    thr = lax.fori_loop(0, 31, thr_step, thr0)
    need = kk - count(lambda key, c: key > thr)
    excess = count(lambda key, c: key == thr) - need
    thr_ref[0] = thr
    cut_ref[0] = jnp.full((1, tq), n_chunks * tkc, I32)
    tie_ref[0, 0] = jnp.broadcast_to(jnp.max(excess, axis=1, keepdims=True), (SUBLANES, LANES))

    @pl.when(jnp.max(excess) > 0)
    def _():
        n_bits = (n_chunks * tkc - 1).bit_length()

        def cut_step(b, cut):
            cand = cut + jnp.left_shift(jnp.int32(1), n_bits - 1 - b)
            below = count(lambda key, c: (key == thr) & (c * tkc + kpos0 < cand))
            return jnp.where(below < need, cand, cut)

        cut_ref[0] = lax.fori_loop(0, n_bits, cut_step, jnp.zeros((1, tq), I32))


def _dsa_index(iqt, iwt, ik_bf, tq, tkc, k_sel):
    b, _, _, s = iqt.shape
    nq = s // tq
    body = functools.partial(_dsa_index_body, tq=tq, tkc=tkc, n_chunks=s // tkc, k_sel=k_sel)
    return pl.pallas_call(
        body, grid=(b, nq), name="dsa_index",
        in_specs=[pl.BlockSpec((1, IDX_HEADS, LANES, tq), lambda bi, i: (bi, 0, 0, i)),
                  pl.BlockSpec((1, IDX_HEADS, tq), lambda bi, i: (bi, 0, i)),
                  pl.BlockSpec((1, s, LANES), lambda bi, i: (bi, 0, 0))],
        out_specs=[pl.BlockSpec((1, 1, s, tq), lambda bi, i: (bi, i, 0, 0)),
                   pl.BlockSpec((1, 1, tq), lambda bi, i: (bi, 0, i)),
                   pl.BlockSpec((1, 1, tq), lambda bi, i: (bi, 0, i)),
                   pl.BlockSpec((1, 1, SUBLANES, LANES), lambda bi, i: (bi, i, 0, 0))],
        out_shape=[jax.ShapeDtypeStruct((b, nq, s, tq), I32),
                   jax.ShapeDtypeStruct((b, 1, s), I32), jax.ShapeDtypeStruct((b, 1, s), I32),
                   jax.ShapeDtypeStruct((b, nq, SUBLANES, LANES), I32)],
        compiler_params=_cparams("parallel", "parallel"),
    )(iqt, iwt, ik_bf)


def _dsa_attn_body(it_ref, jt_ref, tie_ref, qt_ref, k_ref, vt_ref, keys_ref, thr_ref, cut_ref, o_ref,
                   m_sc, acc_sc, *, tq, tk, nq):
    step = pl.program_id(2)
    i = it_ref[step]
    j = jt_ref[step]

    @pl.when(j == 0)
    def _():
        m_sc[...] = jnp.full(m_sc.shape, NEG_BIG, F32)
        acc_sc[...] = jnp.zeros(acc_sc.shape, F32)

    def update(tied):
        k = k_ref[0, 0]
        if tied:
            kpos = j * tk + lax.broadcasted_iota(I32, (tk, tq), 0)
            sel = _selected(keys_ref[0, 0], thr_ref[0], cut_ref[0], kpos)
        else:
            sel = keys_ref[0, 0] >= thr_ref[0]
        logits = [jnp.where(sel, _dot(k, qt_ref[0, r]), -jnp.inf) for r in range(2)]
        _softmax_cols_pair(logits, vt_ref[0, 0], m_sc, acc_sc)

    tied = tie_ref[pl.program_id(0) * nq + i] > 0

    @pl.when(tied)
    def _():
        update(True)

    @pl.when(jnp.logical_not(tied))
    def _():
        update(False)

    @pl.when(j == _last_kblock(i, tq, tk))
    def _():
        _finish_cols(acc_sc, o_ref)


def _dsa_attention(qt, k_hm, vt, keys, thr, cut, tie, tq, tk):
    b, _, _, s = qt.shape
    nq = s // tq
    it, jt = _causal_pairs(nq, tq, tk, descending=False)
    tie_flat = tie[:, :, 0, 0].reshape(b * nq)
    grid_spec = pltpu.PrefetchScalarGridSpec(
        num_scalar_prefetch=3, grid=(b, KV_HEADS, it.shape[0]),
        in_specs=[
            pl.BlockSpec((1, 2, LANES, tq), lambda bi, g, p, it, jt, tie: (bi, g, 0, it[p])),
            pl.BlockSpec((1, 1, tk, LANES), lambda bi, g, p, it, jt, tie: (bi, g, jt[p], 0)),
            pl.BlockSpec((1, 1, LANES, tk), lambda bi, g, p, it, jt, tie: (bi, g, 0, jt[p])),
            pl.BlockSpec((1, 1, tk, tq), lambda bi, g, p, it, jt, tie: (bi, it[p], jt[p], 0)),
            pl.BlockSpec((1, 1, tq), lambda bi, g, p, it, jt, tie: (bi, 0, it[p])),
            pl.BlockSpec((1, 1, tq), lambda bi, g, p, it, jt, tie: (bi, 0, it[p])),
        ],
        out_specs=pl.BlockSpec((1, tq, LANES), lambda bi, g, p, it, jt, tie: (bi, it[p], g)),
        scratch_shapes=[pltpu.VMEM((2, 1, tq), F32), pltpu.VMEM((2, LANES, tq), F32)],
    )
    return pl.pallas_call(
        functools.partial(_dsa_attn_body, tq=tq, tk=tk, nq=nq), grid_spec=grid_spec, name="dsa_attn",
        out_shape=jax.ShapeDtypeStruct((b, s, D_BRANCH), F32),
        compiler_params=_cparams("parallel", "parallel", "arbitrary"),
    )(it, jt, tie_flat, qt, k_hm, vt, keys, thr, cut)


def _merge_body(oa_ref, ob_ref, oc_ref, sza_ref, szb_ref, szc_ref, sg_ref, x_ref, wb_ref, wo_ref, fg_ref,
                out_ref, *, final_norm):
    m = jnp.zeros(x_ref.shape, F32)
    for n, (o_ref, sz_ref) in enumerate(((oa_ref, sza_ref), (ob_ref, szb_ref), (oc_ref, szc_ref))):
        u = (o_ref[...] * sz_ref[...].astype(F32)).astype(BF16)
        y = _dot(u, wb_ref[n])
        m = m + sg_ref[:, n * D_MODEL:(n + 1) * D_MODEL].astype(F32) * y
    r = x_ref[...] + _dot(m.astype(BF16), wo_ref[...])
    if final_norm:
        ms = jnp.mean(r * r, axis=-1, keepdims=True)
        r = r * lax.rsqrt(ms + RMS_EPS) * fg_ref[...]
    out_ref[...] = r


def _merge(oa, ob, oc, sza, szb, szc, sg, x2d, wbr, wo, final_g, tm, final_norm):
    m = x2d.shape[0]
    row = lambda w: pl.BlockSpec((tm, w), lambda i: (i, 0))
    full = lambda a: pl.BlockSpec(a.shape, lambda i: (0,) * a.ndim)
    fg = final_g.reshape(1, D_MODEL)
    return pl.pallas_call(
        functools.partial(_merge_body, final_norm=final_norm), grid=(m // tm,), name="merge",
        in_specs=[row(D_BRANCH)] * 6 + [row(N_BRANCH * D_MODEL), row(D_MODEL), full(wbr), full(wo), full(fg)],
        out_specs=row(D_MODEL), out_shape=jax.ShapeDtypeStruct((m, D_MODEL), F32),
        compiler_params=_cparams("parallel"),
    )(oa, ob, oc, sza, szb, szc, sg, x2d, wbr, wo, fg)


def _paged_specs(block, layer, n_pages):
    zeros = (0,) * (len(block) - 2)
    return [pl.BlockSpec(block, lambda b, pt, k=k: (layer, pt[b * n_pages + k]) + zeros) for k in range(n_pages)]


def _dec_index_body(pt_ref, iq_ref, iw_ref, ikn_ref, *rest, n_pages, page):
    ikt_refs = rest[:n_pages]
    sc_ref, scn_ref = rest[n_pages:]
    iq = iq_ref[0]
    w = iw_ref[0]
    for k in range(n_pages):
        r = jnp.maximum(_dot(iq, ikt_refs[k][0, 0].astype(BF16)), 0.0)
        sc_ref[0, :, k * page:(k + 1) * page] = jnp.sum(w * r, axis=0, keepdims=True)
    rn = jnp.maximum(jnp.sum(iq.astype(F32) * ikn_ref[0], axis=1, keepdims=True), 0.0)
    scn_ref[0] = jnp.broadcast_to(jnp.sum(w * rn, axis=0, keepdims=True), (1, LANES))


def _dec_index(pt_flat, layer, iq_d, iw_d, ik_new, cache_ikt):
    db = iq_d.shape[0]
    n_pages = pt_flat.shape[0] // db
    page = cache_ikt.shape[3]
    per_seq = lambda shape: pl.BlockSpec((1,) + shape, lambda b, pt: (b,) + (0,) * len(shape))
    grid_spec = pltpu.PrefetchScalarGridSpec(
        num_scalar_prefetch=1, grid=(db,),
        in_specs=[per_seq((IDX_HEADS, IDX_DIM)), per_seq((IDX_HEADS, 1)), per_seq((1, IDX_DIM))]
        + _paged_specs((1, 1, IDX_DIM, page), layer, n_pages),
        out_specs=[per_seq((1, n_pages * page)), per_seq((1, LANES))],
    )
    return pl.pallas_call(
        functools.partial(_dec_index_body, n_pages=n_pages, page=page), grid_spec=grid_spec, name="dec_index",
        out_shape=[jax.ShapeDtypeStruct((db, 1, n_pages * page), F32),
                   jax.ShapeDtypeStruct((db, 1, LANES), F32)],
        compiler_params=_cparams("parallel"),
    )(pt_flat, iq_d, iw_d, ik_new, *([cache_ikt] * n_pages))


def _dec_select_body(sc_ref, thr_ref, cut_ref, *, n_keys, k_sel):
    sc = sc_ref[...]
    col = lax.broadcasted_iota(I32, sc.shape, 1)
    keys = jnp.where(col < n_keys, _sortable(sc), INT_MIN)
    count = lambda m: jnp.sum(jnp.where(m, 1, 0).astype(I32), axis=1, keepdims=True)
    thr0 = jnp.where(count(keys >= 0) >= k_sel, 0, INT_MIN).astype(I32)

    def thr_step(b, thr):
        cand = thr + jnp.left_shift(jnp.int32(1), 30 - b)
        return jnp.where(count(keys >= cand) >= k_sel, cand, thr)

    thr = lax.fori_loop(0, 31, thr_step, thr0)
    need = k_sel - count(keys > thr)
    n_bits = (sc.shape[1] - 1).bit_length()

    def cut_step(b, cut):
        cand = cut + jnp.left_shift(jnp.int32(1), n_bits - 1 - b)
        return jnp.where(count((keys == thr) & (col < cand)) < need, cand, cut)

    thr_ref[...] = thr
    cut_ref[...] = lax.fori_loop(0, n_bits, cut_step, jnp.zeros(thr.shape, I32))


def _dec_select(scores, n_keys, k_sel):
    db = scores.shape[0]
    full = lambda shape: pl.BlockSpec(shape, lambda i: (0,) * len(shape))
    return pl.pallas_call(
        functools.partial(_dec_select_body, n_keys=n_keys, k_sel=k_sel), grid=(1,), name="dec_select",
        in_specs=[full(scores.shape)], out_specs=[full((db, 1)), full((db, 1))],
        out_shape=[jax.ShapeDtypeStruct((db, 1), I32), jax.ShapeDtypeStruct((db, 1), I32)],
        compiler_params=_cparams("arbitrary"),
    )(scores)


def _dec_attn_body(pt_ref, qa_ref, qb_ref, qc_ref, bsc_ref, bscn_ref, thr_ref, cut_ref,
                   akn_ref, avn_ref, lfn_ref, bkn_ref, bvn_ref, *rest, n_pages, page):
    groups = [rest[g * n_pages:(g + 1) * n_pages] for g in range(7)]
    akt, avt, lft, bkt, bvt, ckt, cvt = groups
    oa_ref, ob_ref, oc_ref = rest[7 * n_pages:]
    qa = qa_ref[0]
    qb = qb_ref[0]
    qc = qc_ref[0]
    thr = thr_ref[0]
    cut = cut_ref[0]
    tri = _strict_upper_tri(page)
    bf = lambda ref: ref[0, 0].astype(BF16)
    lanes = lambda xs: jnp.concatenate(xs, axis=1)

    def softmax_pv(logits, logit_n, v_pages, v_new):
        m = jnp.maximum(jnp.max(lanes(logits), axis=1, keepdims=True), logit_n)
        pn = jnp.exp2(logit_n - m)
        den = pn
        acc = pn * v_new
        for lg, vt in zip(logits, v_pages):
            p = jnp.exp2(lg - m)
            den = den + jnp.sum(p, axis=1, keepdims=True)
            acc = acc + _dot_nt(p.astype(BF16), bf(vt))
        return acc / den

    after = lfn_ref[0]
    logits = [None] * n_pages
    for k in reversed(range(n_pages)):
        lf = lft[k][0, 0]
        logits[k] = _dot(qa, bf(akt[k])) + (_dot_split(lf, tri) + after) * LOG2E
        after = after + jnp.sum(lf, axis=1, keepdims=True)
    logit_n = jnp.sum(qa.astype(F32) * akn_ref[0], axis=1, keepdims=True)
    oa = softmax_pv(logits, logit_n, avt, avn_ref[0])

    bsc = bsc_ref[0]
    pos = lax.broadcasted_iota(I32, bsc.shape, 1)
    sel = _selected(_sortable(bsc), thr, cut, pos)
    sel_n = _selected(_sortable(bscn_ref[0][:, 0:1]), thr, cut, jnp.int32(n_pages * page))
    logits = [jnp.where(sel[:, k * page:(k + 1) * page], _dot(qb, bf(bkt[k])), -jnp.inf) for k in range(n_pages)]
    logit_n = jnp.where(sel_n, jnp.sum(qb.astype(F32) * bkn_ref[0], axis=1, keepdims=True), -jnp.inf)
    ob = softmax_pv(logits, logit_n, bvt, bvn_ref[0])

    after = jnp.zeros((N_HEADS, 1), F32)
    oc = jnp.zeros((N_HEADS, KV_WIDTH), F32)
    for k in reversed(range(n_pages)):
        z = _dot(qc, bf(ckt[k]))
        sp = _softplus(z)
        a = jnp.exp(z - sp + _dot_split(-sp, tri) + after)
        oc = oc + _dot_nt(a.astype(BF16), bf(cvt[k]))
        after = after - jnp.sum(sp, axis=1, keepdims=True)

    hrow = lax.broadcasted_iota(I32, (N_HEADS, KV_WIDTH), 0)
    lane = lax.broadcasted_iota(I32, (N_HEADS, KV_WIDTH), 1)
    own = jnp.right_shift(lane, 6) == jnp.right_shift(hrow, 1)

    def own_head(acc):
        x = jnp.where(own, acc, 0.0)
        x = x[:, :LANES] + x[:, LANES:]
        return (x + pltpu.roll(x, HEAD_DIM, 1))[:, :HEAD_DIM]

    oa_ref[0] = own_head(oa)
    ob_ref[0] = own_head(ob)
    oc_ref[0] = own_head(oc)


def _dec_attention(pt_flat, layer, qa, qb, qc, caches_t, logf_t, bsc, bscn, thr, cut, new_rows):
    db = qa.shape[0]
    n_pages = pt_flat.shape[0] // db
    akt, avt, bkt, bvt, ckt, cvt = caches_t
    page = akt.shape[3]
    akn, avn, lfn, bkn, bvn = new_rows
    per_seq = lambda shape: pl.BlockSpec((1,) + shape, lambda b, pt: (b,) + (0,) * len(shape))
    kv_pages = lambda: _paged_specs((1, 1, KV_WIDTH, page), layer, n_pages)
    in_specs = [per_seq((N_HEADS, KV_WIDTH))] * 3 + [
        per_seq((1, n_pages * page)), per_seq((1, LANES)), per_seq((1, 1)), per_seq((1, 1)),
        per_seq((1, KV_WIDTH)), per_seq((1, KV_WIDTH)), per_seq((N_HEADS, 1)),
        per_seq((1, KV_WIDTH)), per_seq((1, KV_WIDTH)),
    ] + kv_pages() + kv_pages() + _paged_specs((1, 1, N_HEADS, page), layer, n_pages) \
      + kv_pages() + kv_pages() + kv_pages() + kv_pages()
    grid_spec = pltpu.PrefetchScalarGridSpec(
        num_scalar_prefetch=1, grid=(db,), in_specs=in_specs,
        out_specs=[per_seq((N_HEADS, HEAD_DIM))] * 3,
    )
    out = jax.ShapeDtypeStruct((db, N_HEADS, HEAD_DIM), F32)
    paged_args = []
    for arr in (akt, avt, logf_t, bkt, bvt, ckt, cvt):
        paged_args += [arr] * n_pages
    return pl.pallas_call(
        functools.partial(_dec_attn_body, n_pages=n_pages, page=page), grid_spec=grid_spec, name="dec_attn",
        out_shape=[out, out, out],
        compiler_params=_cparams("parallel"),
    )(pt_flat, qa, qb, qc, bsc, bscn, thr, cut, akn, avn, lfn, bkn, bvn, *paged_args)


def _rope_tables(pos):
    half = HEAD_DIM // 2
    inv_freq = ROPE_THETA ** (-jnp.arange(half, dtype=F32) / half)
    ang = pos.astype(F32)[:, None] * inv_freq[None, :]
    c = jnp.cos(ang)
    s = jnp.sin(ang)
    return jnp.tile(c, (1, 4)), jnp.tile(jnp.concatenate([-s, s], axis=1), (1, 2))


def _pad_cols(w, width):
    return jnp.pad(w, ((0, 0), (0, 0), (0, width - w.shape[-1])))


def _group_weights(w_in):
    splits = np.cumsum(PROJ_SIZES)[:-1].tolist()
    (aq, ak, av, az, af, bq, bk, bv, bz, biq, bik, biw, cq, ck, cv, cz, gates) = jnp.split(w_in, splits, axis=-1)
    wa = jnp.concatenate([aq, az, ak, av, _pad_cols(af, LANES)], axis=-1)
    wb = jnp.concatenate([bq, biq, bz, bk, bv, _pad_cols(bik, LANES), _pad_cols(biw, LANES)], axis=-1)
    wc = jnp.concatenate([cq, cz, ck, cv], axis=-1)
    return wa.astype(BF16), wb.astype(BF16), wc.astype(BF16), gates.astype(BF16)


def _rows_from_cols(qt):
    q = qt[0, :, :HEAD_DIM, :] + qt[0, :, HEAD_DIM:, :]
    return jnp.transpose(q, (2, 0, 1))


def _block_diag(q):
    own = (jnp.arange(N_HEADS)[:, None] // (N_HEADS // KV_HEADS)) == jnp.arange(KV_HEADS)[None, :]
    qbd = jnp.where(own[None, :, :, None], q[:, :, None, :], jnp.zeros((), q.dtype))
    return qbd.reshape(q.shape[0], N_HEADS, KV_WIDTH)


def _layer_prompt(x, lw, tabs, tiles, final_g, final_norm):
    b, s, _ = x.shape
    tm, tq, tk = tiles
    x2d = x.reshape(b * s, D_MODEL)
    pa, pb, pc, sg = _project(x2d, b, s, tm, lw["norm_g"], lw["wa"], lw["wb"], lw["wc"], lw["wg"],
                              lw["bf"], tabs[0], tabs[1])
    logf = pa["logf"].reshape(b, s, N_HEADS)
    dcum_t = _cumsum_heads(jnp.transpose(logf, (0, 2, 1)))
    oa = _fox_attention(pa["qt"], pa["k_hm"], pa["vt"], dcum_t, tq, tk)
    k_sel = max(1, min(TOPK_MAX, s // 4))
    keys, thr, cut, tie = _dsa_index(pb["iqt"], pb["iwt"], pb["ik_bf"].reshape(b, s, LANES), tq, tk, k_sel)
    ob = _dsa_attention(pb["qt"], pb["k_hm"], pb["vt"], keys, thr, cut, tie, tq, tk)
    oc = _sb_attention(pc["q_hm"], pc["k_hm"], pc["v_hm"], tq, tk)
    out = _merge(oa.reshape(b * s, D_BRANCH), ob.reshape(b * s, D_BRANCH), oc.reshape(b * s, D_BRANCH),
                 pa["sz"], pb["sz"], pc["sz"], sg, x2d, lw["w_branch"], lw["w_out"], final_g, tm, final_norm)
    kv = lambda a: a.reshape(b, s, KV_HEADS, HEAD_DIM)
    new = (kv(pa["k"]), kv(pa["v"]), logf, kv(pb["k"]), kv(pb["v"]), pb["ik"].reshape(b, s, IDX_DIM),
           kv(pc["k"]), kv(pc["v"]))
    return out.reshape(b, s, D_MODEL), new


def _layer_sample(x, layer, pt_flat, caches_t, logf_t, cache_ikt, lw, tabs, final_g, final_norm):
    db, t, _ = x.shape
    n_pages = pt_flat.shape[0] // db
    page = caches_t[0].shape[3]
    past = n_pages * page
    x2d = x.reshape(db * t, D_MODEL)
    pa, pb, pc, sg = _project(x2d, 1, db * t, db * t, lw["norm_g"], lw["wa"], lw["wb"], lw["wc"], lw["wg"],
                              lw["bf"], tabs[0], tabs[1])
    sc, scn = _dec_index(pt_flat, layer, _rows_from_cols(pb["iqt"]),
                         jnp.transpose(pb["iwt"], (2, 1, 0)), pb["ik"].reshape(db, 1, IDX_DIM), cache_ikt)
    scores = jnp.concatenate([sc[:, 0, :], scn[:, 0, :]], axis=1)
    k_sel = max(1, min(TOPK_MAX, (past + t) // 4))
    thr, cut = _dec_select(scores, past + t, k_sel)
    new_rows = (pa["k"].reshape(db, 1, KV_WIDTH), pa["v"].reshape(db, 1, KV_WIDTH),
                pa["logf"].reshape(db, N_HEADS, 1),
                pb["k"].reshape(db, 1, KV_WIDTH), pb["v"].reshape(db, 1, KV_WIDTH))
    qc = pc["q_hm"][0, :, :, :HEAD_DIM].transpose(1, 0, 2)
    oa, ob, oc = _dec_attention(pt_flat, layer, _block_diag(_rows_from_cols(pa["qt"])),
                                _block_diag(_rows_from_cols(pb["qt"])), _block_diag(qc),
                                caches_t, logf_t, sc, scn, thr.reshape(db, 1, 1), cut.reshape(db, 1, 1), new_rows)
    out = _merge(oa.reshape(db, D_BRANCH), ob.reshape(db, D_BRANCH), oc.reshape(db, D_BRANCH),
                 pa["sz"], pb["sz"], pc["sz"], sg, x2d, lw["w_branch"], lw["w_out"], final_g, db * t, final_norm)
    kv = lambda a: a.reshape(db, t, KV_HEADS, HEAD_DIM)
    new = (kv(pa["k"]), kv(pa["v"]), pa["logf"].reshape(db, t, N_HEADS), kv(pb["k"]), kv(pb["v"]),
           pb["ik"].reshape(db, t, IDX_DIM), kv(pc["k"]), kv(pc["v"]))
    return out.reshape(db, t, D_MODEL), new


def _prompt_tiles(s):
    t = min(512, s)
    return t, t, t


def kernel(x_prompt, x_sample, cache_a_k, cache_a_v, cache_a_logf, cache_b_k, cache_b_v, cache_b_idx_k,
           cache_c_k, cache_c_v, page_table, norm_g, w_in, b_forget, w_branch, w_out, final_g):
    depth = w_in.shape[0]
    b, s, _ = x_prompt.shape
    db, t, _ = x_sample.shape
    assert t == 1, "decode path handles one new token per sequence"
    n_pool, page = cache_a_k.shape[1], cache_a_k.shape[2]
    past = page_table.shape[1] * page

    wa, wb, wc, wg = _group_weights(w_in)
    bfp = jnp.pad(b_forget, ((0, 0), (0, LANES - N_HEADS))).reshape(depth, 1, LANES)
    wbr = w_branch.astype(BF16)
    wo = w_out.astype(BF16)
    tabs_p = _rope_tables(jnp.arange(s))
    tabs_s = _rope_tables(jnp.full((db,), past))
    page_t = lambda c: jnp.transpose(c, (0, 1, 3, 4, 2)).reshape(depth, n_pool, KV_WIDTH, page)
    caches_t = tuple(page_t(c) for c in (cache_a_k, cache_a_v, cache_b_k, cache_b_v, cache_c_k, cache_c_v))
    logf_t = jnp.transpose(cache_a_logf, (0, 1, 3, 2))
    cache_ikt = jnp.transpose(cache_b_idx_k, (0, 1, 3, 2))
    pt_flat = page_table.reshape(-1)

    xp, xs = x_prompt, x_sample
    p_rows, s_rows = [], []
    for l in range(depth):
        lw = dict(norm_g=norm_g[l], wa=wa[l], wb=wb[l], wc=wc[l], wg=wg[l], bf=bfp[l],
                  w_branch=wbr[l], w_out=wo[l])
        last = l == depth - 1
        xp, pn = _layer_prompt(xp, lw, tabs_p, _prompt_tiles(s), final_g, last)
        xs, sn = _layer_sample(xs, l, pt_flat, caches_t, logf_t, cache_ikt, lw, tabs_s, final_g, last)
        p_rows.append(pn)
        s_rows.append(sn)
    p_out = [jnp.stack([r[i] for r in p_rows]) for i in range(8)]
    s_out = [jnp.stack([r[i] for r in s_rows]) for i in range(8)]
    return (xp, xs, *p_out, *s_out)
```

```python
import functools

import numpy as np
import jax
import jax.numpy as jnp
from jax import lax
from jax.experimental import pallas as pl
from jax.experimental.pallas import tpu as pltpu

F32 = jnp.float32
BF16 = jnp.bfloat16
I32 = jnp.int32

D_MODEL = 1024
HEAD_DIM = 64
N_HEADS = 8
KV_HEADS = 4
D_BRANCH = N_HEADS * HEAD_DIM
KV_WIDTH = KV_HEADS * HEAD_DIM
N_BRANCH = 3
IDX_HEADS = 8
IDX_DIM = 64
TOPK_MAX = 256
ROPE_THETA = 10000.0
RMS_EPS = 1e-6

PROJ_SIZES = (
    D_BRANCH, KV_WIDTH, KV_WIDTH, D_BRANCH, N_HEADS,
    D_BRANCH, KV_WIDTH, KV_WIDTH, D_BRANCH, IDX_HEADS * IDX_DIM, IDX_DIM, IDX_HEADS,
    D_BRANCH, KV_WIDTH, KV_WIDTH, D_BRANCH,
    N_BRANCH * D_MODEL,
)

LANES = 128
SUBLANES = 8
CUMSUM_CHUNK = 256
KEY_SUB = 1024
VMEM_LIMIT_BYTES = 56 * 1024 * 1024
NEG_BIG = -1e30
INT_MIN = -(2 ** 31)
LOG2E = 1.4426950408889634
QK_SCALE2 = HEAD_DIM ** -0.5 * LOG2E


def _cparams(*sem):
    return pltpu.CompilerParams(dimension_semantics=sem, vmem_limit_bytes=VMEM_LIMIT_BYTES)


def _softplus(x):
    return jnp.maximum(x, 0.0) + jnp.log(1.0 + jnp.exp(-jnp.abs(x)))


def _dot(a, b):
    return jnp.dot(a, b, preferred_element_type=F32)


def _dot_nt(a, b):
    return lax.dot_general(a, b, (((1,), (1,)), ((), ())), preferred_element_type=F32)


def _dot_split(x, tri):
    hi = x.astype(BF16)
    lo = (x - hi.astype(F32)).astype(BF16)
    return _dot(hi, tri) + _dot(lo, tri)


def _sortable(x):
    x = jnp.where(x == 0.0, 0.0, x)
    bits = pltpu.bitcast(x, I32)
    return jnp.where(bits >= 0, bits, bits ^ 0x7FFFFFFF)


def _selected(key, thr, cut, pos):
    return (key > thr) | ((key == thr) & (pos <= cut))


def _strict_upper_tri(n):
    r = lax.broadcasted_iota(I32, (n, n), 0)
    c = lax.broadcasted_iota(I32, (n, n), 1)
    return (r > c).astype(BF16)


def _rms_bf16(x_ref, g_ref):
    xf = x_ref[...]
    ms = jnp.mean(xf * xf, axis=-1, keepdims=True)
    return (xf * lax.rsqrt(ms + RMS_EPS) * g_ref[...]).astype(BF16)


def _rope128(x, cos, sin):
    lane = lax.broadcasted_iota(I32, x.shape, 1)
    partner = jnp.where((lane & 32) == 0, pltpu.roll(x, 96, 1), pltpu.roll(x, 32, 1))
    return x * cos + partner * sin


def _pair_lower_upper(blk):
    lane = lax.broadcasted_iota(I32, blk.shape, 1)
    swapped = pltpu.roll(blk, 64, 1)
    low = lane < HEAD_DIM
    return jnp.where(low, blk, swapped), jnp.where(low, swapped, blk)


def _store_q_rows(ref, blk, c):
    lane = lax.broadcasted_iota(I32, blk.shape, 1)
    low = lane < HEAD_DIM
    a, b = _pair_lower_upper(blk)
    ref[0, 2 * c] = jnp.where(low, a, 0.0).astype(ref.dtype)
    ref[0, 2 * c + 1] = jnp.where(low, b, 0.0).astype(ref.dtype)


def _store_kv_rows(ref, blk, c):
    a, b = _pair_lower_upper(blk)
    ref[0, 2 * c] = a.astype(ref.dtype)
    ref[0, 2 * c + 1] = b.astype(ref.dtype)


def _store_q_cols(ref, blk, c):
    t = blk.T
    row = lax.broadcasted_iota(I32, t.shape, 0)
    ref[0, 2 * c] = jnp.where(row < HEAD_DIM, t, 0.0).astype(ref.dtype)
    ref[0, 2 * c + 1] = jnp.where(row < HEAD_DIM, 0.0, t).astype(ref.dtype)


def _store_v_cols(ref, t, blk, c):
    ts = pltpu.roll(blk, 64, 1).T
    row = lax.broadcasted_iota(I32, t.shape, 0)
    ref[0, 2 * c] = jnp.where(row < HEAD_DIM, t, 1.0).astype(ref.dtype)
    ref[0, 2 * c + 1] = jnp.where(row < HEAD_DIM, ts, 1.0).astype(ref.dtype)


def _store_v_rows(ref, t, blk, c):
    _store_kv_rows(ref, blk, c)


def _store_kv(k_blk, v_blk, c, kt_out, vt_out, k_hm, v_out_fn):
    rows = slice(c * LANES, (c + 1) * LANES)
    kt_out[0, rows, :] = k_blk.T
    vt = v_blk.T
    vt_out[0, rows, :] = vt
    _store_kv_rows(k_hm, k_blk, c)
    v_out_fn(vt, v_blk, c)


def _proj_a_body(x_ref, g_ref, w_ref, bf_ref, qt_out, kt_out, k_hm, vt_out, vt_bf, sz_out, logft_out):
    h = _rms_bf16(x_ref, g_ref)
    q = _dot(h, w_ref[:, 0:512]) * QK_SCALE2
    for c in range(4):
        _store_q_cols(qt_out, q[:, c * LANES:(c + 1) * LANES], c)
    z = _dot(h, w_ref[:, 512:1024])
    sz_out[...] = (z * jax.nn.sigmoid(z)).astype(BF16)
    k = _dot(h, w_ref[:, 1024:1280])
    v = _dot(h, w_ref[:, 1280:1536])
    for c in range(2):
        sl = slice(c * LANES, (c + 1) * LANES)
        _store_kv(k[:, sl], v[:, sl], c, kt_out, vt_out, k_hm, functools.partial(_store_v_cols, vt_bf))
    f = _dot(h, w_ref[:, 1536:1664]) + bf_ref[...]
    logft_out[0] = (-_softplus(-f)).T[:N_HEADS]


def _proj_b_body(x_ref, g_ref, w_ref, cos_ref, sin_ref,
                 qt_out, iqt_out, sz_out, kt_out, k_hm, vt_out, vt_bf, ikt_out, ik_bf, iwt_out):
    h = _rms_bf16(x_ref, g_ref)
    cos = cos_ref[...]
    sin = sin_ref[...]
    q = _dot(h, w_ref[:, 0:512])
    iq = _dot(h, w_ref[:, 512:1024])
    for c in range(4):
        sl = slice(c * LANES, (c + 1) * LANES)
        _store_q_cols(qt_out, _rope128(q[:, sl], cos, sin) * QK_SCALE2, c)
        _store_q_cols(iqt_out, _rope128(iq[:, sl], cos, sin) * (IDX_DIM ** -0.5), c)
    z = _dot(h, w_ref[:, 1024:1536])
    sz_out[...] = (z * jax.nn.sigmoid(z)).astype(BF16)
    k = _dot(h, w_ref[:, 1536:1792])
    v = _dot(h, w_ref[:, 1792:2048])
    for c in range(2):
        sl = slice(c * LANES, (c + 1) * LANES)
        _store_kv(_rope128(k[:, sl], cos, sin), v[:, sl], c, kt_out, vt_out, k_hm,
                  functools.partial(_store_v_cols, vt_bf))
    ik = _rope128(_dot(h, w_ref[:, 2048:2176]), cos, sin)
    ikt_out[0] = ik.T[:IDX_DIM]
    ik_bf[...] = (ik + pltpu.roll(ik, 64, 1)).astype(BF16)
    iw = _dot(h, w_ref[:, 2176:2304]) * (IDX_HEADS ** -0.5)
    iwt_out[0] = iw.T[:IDX_HEADS]


def _proj_c_body(x_ref, g_ref, w_ref, q_hm, sz_out, kt_out, k_hm, vt_out, v_hm):
    h = _rms_bf16(x_ref, g_ref)
    q = _dot(h, w_ref[:, 0:512]) * (HEAD_DIM ** -0.5)
    for c in range(4):
        _store_q_rows(q_hm, q[:, c * LANES:(c + 1) * LANES], c)
    z = _dot(h, w_ref[:, 512:1024])
    sz_out[...] = (z * jax.nn.sigmoid(z)).astype(BF16)
    k = _dot(h, w_ref[:, 1024:1280])
    v = _dot(h, w_ref[:, 1280:1536])
    for c in range(2):
        sl = slice(c * LANES, (c + 1) * LANES)
        _store_kv(k[:, sl], v[:, sl], c, kt_out, vt_out, k_hm, functools.partial(_store_v_rows, v_hm))


def _proj_g_body(x_ref, g_ref, w_ref, sg_out):
    h = _rms_bf16(x_ref, g_ref)
    for c in range(N_BRANCH):
        sl = slice(c * D_MODEL, (c + 1) * D_MODEL)
        sg_out[:, sl] = jax.nn.sigmoid(_dot(h, w_ref[:, sl])).astype(BF16)


def _project(x2d, bx, sx, tm, norm_g, wa, wb, wc, wg, bfp, cos_rows, sin_rows):
    m = bx * sx
    ns = sx // tm
    grid = (m // tm,)
    row = lambda w: pl.BlockSpec((tm, w), lambda i: (i, 0))
    full = lambda a: pl.BlockSpec(a.shape, lambda i: (0,) * a.ndim)
    hm = lambda nh: pl.BlockSpec((1, nh, tm, LANES), lambda i: (i // ns, 0, i % ns, 0))
    hm_shape = lambda nh: jax.ShapeDtypeStruct((bx, nh, sx, LANES), BF16)
    ht = lambda nh: pl.BlockSpec((1, nh, LANES, tm), lambda i: (i // ns, 0, 0, i % ns))
    ht_shape = lambda nh: jax.ShapeDtypeStruct((bx, nh, LANES, sx), BF16)
    rows = lambda w, dt: jax.ShapeDtypeStruct((m, w), dt)
    cm = lambda ch: pl.BlockSpec((1, ch, tm), lambda i: (i // ns, 0, i % ns))
    cm_shape = lambda ch: jax.ShapeDtypeStruct((bx, ch, sx), F32)
    tab = pl.BlockSpec((tm, LANES), lambda i: (i % ns, 0))
    g2 = norm_g.reshape(1, D_MODEL)

    a = pl.pallas_call(
        _proj_a_body, grid=grid, name="proj_a",
        in_specs=[row(D_MODEL), full(g2), full(wa), full(bfp)],
        out_specs=[ht(N_HEADS), cm(KV_WIDTH), hm(KV_HEADS), cm(KV_WIDTH), ht(KV_HEADS),
                   row(D_BRANCH), cm(N_HEADS)],
        out_shape=[ht_shape(N_HEADS), cm_shape(KV_WIDTH), hm_shape(KV_HEADS), cm_shape(KV_WIDTH),
                   ht_shape(KV_HEADS), rows(D_BRANCH, BF16), cm_shape(N_HEADS)],
        compiler_params=_cparams("parallel"),
    )(x2d, g2, wa, bfp)
    b = pl.pallas_call(
        _proj_b_body, grid=grid, name="proj_b",
        in_specs=[row(D_MODEL), full(g2), full(wb), tab, tab],
        out_specs=[ht(N_HEADS), ht(IDX_HEADS), row(D_BRANCH), cm(KV_WIDTH), hm(KV_HEADS),
                   cm(KV_WIDTH), ht(KV_HEADS), cm(IDX_DIM), row(LANES), cm(IDX_HEADS)],
        out_shape=[ht_shape(N_HEADS), ht_shape(IDX_HEADS), rows(D_BRANCH, BF16), cm_shape(KV_WIDTH),
                   hm_shape(KV_HEADS), cm_shape(KV_WIDTH), ht_shape(KV_HEADS), cm_shape(IDX_DIM),
                   rows(LANES, BF16), cm_shape(IDX_HEADS)],
        compiler_params=_cparams("parallel"),
    )(x2d, g2, wb, cos_rows, sin_rows)
    c = pl.pallas_call(
        _proj_c_body, grid=grid, name="proj_c",
        in_specs=[row(D_MODEL), full(g2), full(wc)],
        out_specs=[hm(N_HEADS), row(D_BRANCH), cm(KV_WIDTH), hm(KV_HEADS), cm(KV_WIDTH), hm(KV_HEADS)],
        out_shape=[hm_shape(N_HEADS), rows(D_BRANCH, BF16), cm_shape(KV_WIDTH), hm_shape(KV_HEADS),
                   cm_shape(KV_WIDTH), hm_shape(KV_HEADS)],
        compiler_params=_cparams("parallel"),
    )(x2d, g2, wc)
    sg = pl.pallas_call(
        _proj_g_body, grid=grid, name="proj_g",
        in_specs=[row(D_MODEL), full(g2), full(wg)],
        out_specs=row(N_BRANCH * D_MODEL),
        out_shape=rows(N_BRANCH * D_MODEL, BF16),
        compiler_params=_cparams("parallel"),
    )(x2d, g2, wg)
    keys_a = ("qt", "kt", "k_hm", "vt", "vt_bf", "sz", "logft")
    keys_b = ("qt", "iqt", "sz", "kt", "k_hm", "vt", "vt_bf", "ikt", "ik_bf", "iwt")
    keys_c = ("q_hm", "sz", "kt", "k_hm", "vt", "v_hm")
    return dict(zip(keys_a, a)), dict(zip(keys_b, b)), dict(zip(keys_c, c)), sg


def _cumsum_body(x_ref, o_ref, *, n_chunks):
    r = lax.broadcasted_iota(I32, (CUMSUM_CHUNK, CUMSUM_CHUNK), 0)
    c = lax.broadcasted_iota(I32, (CUMSUM_CHUNK, CUMSUM_CHUNK), 1)
    tri = (r <= c).astype(F32)
    carry = jnp.zeros((N_HEADS, 1), F32)
    for ch in range(n_chunks):
        sl = slice(ch * CUMSUM_CHUNK, (ch + 1) * CUMSUM_CHUNK)
        inc = jnp.dot(x_ref[0, :, sl], tri, preferred_element_type=F32,
                      precision=lax.Precision.HIGHEST) + carry
        o_ref[0, :, sl] = inc * LOG2E
        carry = inc[:, CUMSUM_CHUNK - 1:CUMSUM_CHUNK]


def _cumsum_heads(logf_t):
    b, nh, s = logf_t.shape
    spec = pl.BlockSpec((1, nh, s), lambda i: (i, 0, 0))
    return pl.pallas_call(
        functools.partial(_cumsum_body, n_chunks=s // CUMSUM_CHUNK), grid=(b,), name="logf_cumsum",
        in_specs=[spec], out_specs=spec, out_shape=jax.ShapeDtypeStruct(logf_t.shape, F32),
        compiler_params=_cparams("parallel"),
    )(logf_t)


def _causal_pairs(nq, tq, tk, descending):
    it, jt = [], []
    for i in range(nq):
        jmax = ((i + 1) * tq - 1) // tk
        js = range(jmax, -1, -1) if descending else range(jmax + 1)
        for j in js:
            it.append(i)
            jt.append(j)
    return jnp.asarray(np.asarray(it, np.int32)), jnp.asarray(np.asarray(jt, np.int32))


def _last_kblock(i, tq, tk):
    return ((i + 1) * tq - 1) // tk


def _finish_cols(acc_sc, o_ref):
    a0 = acc_sc[0]
    a1 = acc_sc[1]
    ot = jnp.concatenate([a0[:HEAD_DIM] / a0[HEAD_DIM:], a1[:HEAD_DIM] / a1[HEAD_DIM:]], axis=0)
    o_ref[0] = ot.T


def _softmax_cols_pair(logits, vt, m_sc, acc_sc):
    m_old = [m_sc[0], m_sc[1]]
    acc_old = [acc_sc[0], acc_sc[1]]
    for r in range(2):
        m_new = jnp.maximum(m_old[r], jnp.max(logits[r], axis=0, keepdims=True))
        p = jnp.exp2(logits[r] - m_new)
        acc_sc[r] = jnp.exp2(m_old[r] - m_new) * acc_old[r] + _dot(vt, p.astype(BF16))
        m_sc[r] = m_new


def _fox_body(it_ref, jt_ref, qt_ref, k_ref, vt_ref, dq_ref, dk_ref, o_ref, m_sc, acc_sc, *, tq, tk):
    step = pl.program_id(2)
    i = it_ref[step]
    j = jt_ref[step]

    @pl.when(j == 0)
    def _():
        m_sc[...] = jnp.full(m_sc.shape, NEG_BIG, F32)
        acc_sc[...] = jnp.zeros(acc_sc.shape, F32)

    def update(masked):
        sub = min(KEY_SUB, tk)
        for c in range(tk // sub):
            ks = slice(c * sub, (c + 1) * sub)
            k = k_ref[0, 0, ks, :]
            logits = []
            for r in range(2):
                dk = jnp.tile(dk_ref[0, r, ks, :], (1, tq // LANES))
                logits.append(_dot(k, qt_ref[0, r]) + (dq_ref[0, r] - dk))
            if masked:
                kpos = j * tk + c * sub + lax.broadcasted_iota(I32, (sub, tq), 0)
                qpos = i * tq + lax.broadcasted_iota(I32, (sub, tq), 1)
                logits = [jnp.where(kpos <= qpos, lg, -jnp.inf) for lg in logits]
            _softmax_cols_pair(logits, vt_ref[0, 0, :, ks], m_sc, acc_sc)

    all_visible = (j + 1) * tk - 1 <= i * tq

    @pl.when(all_visible)
    def _():
        update(False)

    @pl.when(jnp.logical_not(all_visible))
    def _():
        update(True)

    @pl.when(j == _last_kblock(i, tq, tk))
    def _():
        _finish_cols(acc_sc, o_ref)


def _fox_attention(qt, k_hm, vt, dcum_t, tq, tk):
    b, _, _, s = qt.shape
    it, jt = _causal_pairs(s // tq, tq, tk, descending=False)
    dq4 = dcum_t.reshape(b, N_HEADS, 1, s)
    dk4 = jnp.broadcast_to(dcum_t[..., None], (b, N_HEADS, s, LANES))
    grid_spec = pltpu.PrefetchScalarGridSpec(
        num_scalar_prefetch=2, grid=(b, KV_HEADS, it.shape[0]),
        in_specs=[
            pl.BlockSpec((1, 2, LANES, tq), lambda bi, g, p, it, jt: (bi, g, 0, it[p])),
            pl.BlockSpec((1, 1, tk, LANES), lambda bi, g, p, it, jt: (bi, g, jt[p], 0)),
            pl.BlockSpec((1, 1, LANES, tk), lambda bi, g, p, it, jt: (bi, g, 0, jt[p])),
            pl.BlockSpec((1, 2, 1, tq), lambda bi, g, p, it, jt: (bi, g, 0, it[p])),
            pl.BlockSpec((1, 2, tk, LANES), lambda bi, g, p, it, jt: (bi, g, jt[p], 0)),
        ],
        out_specs=pl.BlockSpec((1, tq, LANES), lambda bi, g, p, it, jt: (bi, it[p], g)),
        scratch_shapes=[pltpu.VMEM((2, 1, tq), F32), pltpu.VMEM((2, LANES, tq), F32)],
    )
    return pl.pallas_call(
        functools.partial(_fox_body, tq=tq, tk=tk), grid_spec=grid_spec, name="fox_attn",
        out_shape=jax.ShapeDtypeStruct((b, s, D_BRANCH), F32),
        compiler_params=_cparams("parallel", "parallel", "arbitrary"),
    )(it, jt, qt, k_hm, vt, dq4, dk4)


def _sb_body(it_ref, jt_ref, q_ref, k_ref, v_ref, o_ref, after_sc, acc_sc, *, tq, tk):
    step = pl.program_id(2)
    i = it_ref[step]
    j = jt_ref[step]

    @pl.when(j == _last_kblock(i, tq, tk))
    def _():
        after_sc[...] = jnp.zeros(after_sc.shape, F32)
        acc_sc[...] = jnp.zeros(acc_sc.shape, F32)

    sub = min(CUMSUM_CHUNK, tk)

    def update(masked):
        q2 = q_ref[0].reshape(2 * tq, LANES)
        z = _dot_nt(q2, k_ref[0, 0])
        v = v_ref[0, 0]
        sp = _softplus(z)
        log_sig = z - sp
        if masked:
            rr = lax.broadcasted_iota(I32, (2 * tq, tk), 0)
            row = i * tq + jnp.where(rr >= tq, rr - tq, rr)
            col = j * tk + lax.broadcasted_iota(I32, (2 * tq, tk), 1)
            valid = col < row
            sp = jnp.where(valid, sp, 0.0)
        r = lax.broadcasted_iota(I32, (sub, sub + LANES), 0)
        c = lax.broadcasted_iota(I32, (sub, sub + LANES), 1)
        neg_tri_ones = jnp.where((r > c) | (c >= sub), -1.0, 0.0).astype(BF16)
        after = after_sc[...]
        acc = acc_sc[...]
        for cidx in reversed(range(tk // sub)):
            cs = slice(cidx * sub, (cidx + 1) * sub)
            sums = _dot(sp[:, cs].astype(BF16), neg_tri_ones)
            a = jnp.exp(log_sig[:, cs] + sums[:, :sub] + jnp.tile(after, (1, sub // LANES)))
            if masked:
                a = jnp.where(valid[:, cs], a, 0.0)
            acc = acc + _dot(a.astype(BF16), v[cs])
            after = after + sums[:, sub:]
        after_sc[...] = after
        acc_sc[...] = acc

    all_valid = (j + 1) * tk - 1 < i * tq

    @pl.when(all_valid)
    def _():
        update(False)

    @pl.when(jnp.logical_not(all_valid))
    def _():
        update(True)

    @pl.when(j == 0)
    def _():
        acc = acc_sc[...]
        lane = lax.broadcasted_iota(I32, (tq, LANES), 1)
        o_ref[0] = jnp.where(lane < HEAD_DIM, acc[:tq], acc[tq:])


def _sb_attention(q_hm, k_hm, v_hm, tq, tk):
    b, _, s, _ = q_hm.shape
    it, jt = _causal_pairs(s // tq, tq, tk, descending=True)
    grid_spec = pltpu.PrefetchScalarGridSpec(
        num_scalar_prefetch=2, grid=(b, KV_HEADS, it.shape[0]),
        in_specs=[
            pl.BlockSpec((1, 2, tq, LANES), lambda bi, g, p, it, jt: (bi, g, it[p], 0)),
            pl.BlockSpec((1, 1, tk, LANES), lambda bi, g, p, it, jt: (bi, g, jt[p], 0)),
            pl.BlockSpec((1, 1, tk, LANES), lambda bi, g, p, it, jt: (bi, g, jt[p], 0)),
        ],
        out_specs=pl.BlockSpec((1, tq, LANES), lambda bi, g, p, it, jt: (bi, it[p], g)),
        scratch_shapes=[pltpu.VMEM((2 * tq, LANES), F32), pltpu.VMEM((2 * tq, LANES), F32)],
    )
    return pl.pallas_call(
        functools.partial(_sb_body, tq=tq, tk=tk), grid_spec=grid_spec, name="sb_attn",
        out_shape=jax.ShapeDtypeStruct((b, s, D_BRANCH), F32),
        compiler_params=_cparams("parallel", "parallel", "arbitrary"),
    )(it, jt, q_hm, k_hm, v_hm)


def _dsa_index_body(iqt_ref, iwt_ref, ik_ref, keys_ref, thr_ref, cut_ref, tie_ref, *, tq, tkc, n_chunks, k_sel):
    i = pl.program_id(1)
    n_c = _last_kblock(i, tq, tkc) + 1
    w = iwt_ref[0]
    kpos0 = lax.broadcasted_iota(I32, (tkc, tq), 0)
    qpos = i * tq + lax.broadcasted_iota(I32, (tkc, tq), 1)

    def chunk(c):
        return pl.ds(pl.multiple_of(c * tkc, tkc), tkc)

    def score_chunk(c, carry):
        ikc = ik_ref[0, chunk(c), :]
        acc = jnp.zeros((tkc, tq), F32)
        for h in range(IDX_HEADS):
            acc = acc + w[h:h + 1, :] * jnp.maximum(_dot(ikc, iqt_ref[0, h]), 0.0)
        keys_ref[0, 0, chunk(c), :] = jnp.where(c * tkc + kpos0 <= qpos, _sortable(acc), INT_MIN)
        return carry

    lax.fori_loop(0, n_c, score_chunk, 0)

    def fill_chunk(c, carry):
        keys_ref[0, 0, chunk(c), :] = jnp.full((tkc, tq), INT_MIN, I32)
        return carry

    lax.fori_loop(n_c, n_chunks, fill_chunk, 0)

    def count(pred):
        def body(c, acc):
            hit = jnp.where(pred(keys_ref[0, 0, chunk(c), :], c), 1, 0).astype(I32)
            return acc + jnp.sum(hit.reshape(tkc // SUBLANES, SUBLANES, tq), axis=0)
        acc = lax.fori_loop(0, n_c, body, jnp.zeros((SUBLANES, tq), I32))
        return jnp.sum(acc, axis=0, keepdims=True)

    kk = jnp.minimum(qpos[0:1, :] + 1, k_sel)

    thr0 = jnp.where(count(lambda key, c: key >= 0) >= kk, 0, INT_MIN).astype(I32)

    def thr_step(b, thr):
        cand = thr + jnp.left_shift(jnp.int32(1), 30 - b)
        return jnp.where(count(lambda key, c: key >= cand) >= kk, cand, thr)

    thr = lax.fori_loop(0, 31, thr_step, thr0)
    need = kk - count(lambda key, c: key > thr)
    excess = count(lambda key, c: key == thr) - need
    thr_ref[0] = thr
    cut_ref[0] = jnp.full((1, tq), n_chunks * tkc, I32)
    tie_ref[0, 0] = jnp.broadcast_to(jnp.max(excess, axis=1, keepdims=True), (SUBLANES, LANES))

    @pl.when(jnp.max(excess) > 0)
    def _():
        n_bits = (n_chunks * tkc - 1).bit_length()

        def cut_step(b, cut):
            cand = cut + jnp.left_shift(jnp.int32(1), n_bits - 1 - b)
            below = count(lambda key, c: (key == thr) & (c * tkc + kpos0 < cand))
            return jnp.where(below < need, cand, cut)

        cut_ref[0] = lax.fori_loop(0, n_bits, cut_step, jnp.zeros((1, tq), I32))


def _dsa_index(iqt, iwt, ik_bf, tq, tkc, k_sel):
    b, _, _, s = iqt.shape
    nq = s // tq
    body = functools.partial(_dsa_index_body, tq=tq, tkc=tkc, n_chunks=s // tkc, k_sel=k_sel)
    return pl.pallas_call(
        body, grid=(b, nq), name="dsa_index",
        in_specs=[pl.BlockSpec((1, IDX_HEADS, LANES, tq), lambda bi, i: (bi, 0, 0, i)),
                  pl.BlockSpec((1, IDX_HEADS, tq), lambda bi, i: (bi, 0, i)),
                  pl.BlockSpec((1, s, LANES), lambda bi, i: (bi, 0, 0))],
        out_specs=[pl.BlockSpec((1, 1, s, tq), lambda bi, i: (bi, i, 0, 0)),
                   pl.BlockSpec((1, 1, tq), lambda bi, i: (bi, 0, i)),
                   pl.BlockSpec((1, 1, tq), lambda bi, i: (bi, 0, i)),
                   pl.BlockSpec((1, 1, SUBLANES, LANES), lambda bi, i: (bi, i, 0, 0))],
        out_shape=[jax.ShapeDtypeStruct((b, nq, s, tq), I32),
                   jax.ShapeDtypeStruct((b, 1, s), I32), jax.ShapeDtypeStruct((b, 1, s), I32),
                   jax.ShapeDtypeStruct((b, nq, SUBLANES, LANES), I32)],
        compiler_params=_cparams("parallel", "parallel"),
    )(iqt, iwt, ik_bf)


def _dsa_attn_body(it_ref, jt_ref, tie_ref, qt_ref, k_ref, vt_ref, keys_ref, thr_ref, cut_ref, o_ref,
                   m_sc, acc_sc, *, tq, tk, nq):
    step = pl.program_id(2)
    i = it_ref[step]
    j = jt_ref[step]

    @pl.when(j == 0)
    def _():
        m_sc[...] = jnp.full(m_sc.shape, NEG_BIG, F32)
        acc_sc[...] = jnp.zeros(acc_sc.shape, F32)

    n_idx = keys_ref.shape[1]

    def update(tied):
        k = k_ref[0, 0]
        key = jnp.concatenate([keys_ref[0, h] for h in range(n_idx)], axis=1)
        if tied:
            kpos = j * tk + lax.broadcasted_iota(I32, (tk, tq), 0)
            sel = _selected(key, thr_ref[0], cut_ref[0], kpos)
        else:
            sel = key >= thr_ref[0]
        logits = [jnp.where(sel, _dot(k, qt_ref[0, r]), -jnp.inf) for r in range(2)]
        _softmax_cols_pair(logits, vt_ref[0, 0], m_sc, acc_sc)

    tie0 = (pl.program_id(0) * nq + i) * n_idx
    tied = tie_ref[tie0] > 0
    for h in range(1, n_idx):
        tied = jnp.logical_or(tied, tie_ref[tie0 + h] > 0)

    @pl.when(tied)
    def _():
        update(True)

    @pl.when(jnp.logical_not(tied))
    def _():
        update(False)

    @pl.when(j == _last_kblock(i, tq, tk))
    def _():
        _finish_cols(acc_sc, o_ref)


def _dsa_attention(qt, k_hm, vt, keys, thr, cut, tie, tq, tk):
    b, _, _, s = qt.shape
    nq = s // tq
    tq_idx = keys.shape[3]
    n_idx = tq // tq_idx
    it, jt = _causal_pairs(nq, tq, tk, descending=False)
    tie_flat = tie[:, :, 0, 0].reshape(b * nq * n_idx)
    grid_spec = pltpu.PrefetchScalarGridSpec(
        num_scalar_prefetch=3, grid=(b, KV_HEADS, it.shape[0]),
        in_specs=[
            pl.BlockSpec((1, 2, LANES, tq), lambda bi, g, p, it, jt, tie: (bi, g, 0, it[p])),
            pl.BlockSpec((1, 1, tk, LANES), lambda bi, g, p, it, jt, tie: (bi, g, jt[p], 0)),
            pl.BlockSpec((1, 1, LANES, tk), lambda bi, g, p, it, jt, tie: (bi, g, 0, jt[p])),
            pl.BlockSpec((1, n_idx, tk, tq_idx), lambda bi, g, p, it, jt, tie: (bi, it[p], jt[p], 0)),
            pl.BlockSpec((1, 1, tq), lambda bi, g, p, it, jt, tie: (bi, 0, it[p])),
            pl.BlockSpec((1, 1, tq), lambda bi, g, p, it, jt, tie: (bi, 0, it[p])),
        ],
        out_specs=pl.BlockSpec((1, tq, LANES), lambda bi, g, p, it, jt, tie: (bi, it[p], g)),
        scratch_shapes=[pltpu.VMEM((2, 1, tq), F32), pltpu.VMEM((2, LANES, tq), F32)],
    )
    return pl.pallas_call(
        functools.partial(_dsa_attn_body, tq=tq, tk=tk, nq=nq), grid_spec=grid_spec, name="dsa_attn",
        out_shape=jax.ShapeDtypeStruct((b, s, D_BRANCH), F32),
        compiler_params=_cparams("parallel", "parallel", "arbitrary"),
    )(it, jt, tie_flat, qt, k_hm, vt, keys, thr, cut)


def _merge_body(oa_ref, ob_ref, oc_ref, sza_ref, szb_ref, szc_ref, sg_ref, x_ref, wb_ref, wo_ref, fg_ref,
                out_ref, *, final_norm):
    m = jnp.zeros(x_ref.shape, F32)
    for n, (o_ref, sz_ref) in enumerate(((oa_ref, sza_ref), (ob_ref, szb_ref), (oc_ref, szc_ref))):
        u = (o_ref[...] * sz_ref[...].astype(F32)).astype(BF16)
        y = _dot(u, wb_ref[n])
        m = m + sg_ref[:, n * D_MODEL:(n + 1) * D_MODEL].astype(F32) * y
    r = x_ref[...] + _dot(m.astype(BF16), wo_ref[...])
    if final_norm:
        ms = jnp.mean(r * r, axis=-1, keepdims=True)
        r = r * lax.rsqrt(ms + RMS_EPS) * fg_ref[...]
    out_ref[...] = r


def _merge(oa, ob, oc, sza, szb, szc, sg, x2d, wbr, wo, final_g, tm, final_norm):
    m = x2d.shape[0]
    row = lambda w: pl.BlockSpec((tm, w), lambda i: (i, 0))
    full = lambda a: pl.BlockSpec(a.shape, lambda i: (0,) * a.ndim)
    fg = final_g.reshape(1, D_MODEL)
    return pl.pallas_call(
        functools.partial(_merge_body, final_norm=final_norm), grid=(m // tm,), name="merge",
        in_specs=[row(D_BRANCH)] * 6 + [row(N_BRANCH * D_MODEL), row(D_MODEL), full(wbr), full(wo), full(fg)],
        out_specs=row(D_MODEL), out_shape=jax.ShapeDtypeStruct((m, D_MODEL), F32),
        compiler_params=_cparams("parallel"),
    )(oa, ob, oc, sza, szb, szc, sg, x2d, wbr, wo, fg)


def _paged_specs(block, layer, n_pages):
    zeros = (0,) * (len(block) - 2)
    return [pl.BlockSpec(block, lambda b, pt, k=k: (layer, pt[b * n_pages + k]) + zeros) for k in range(n_pages)]


def _dec_index_body(pt_ref, iq_ref, iw_ref, ikn_ref, *rest, n_pages, page):
    ikt_refs = rest[:n_pages]
    sc_ref, scn_ref = rest[n_pages:]
    iq = iq_ref[0]
    w = iw_ref[0]
    for k in range(n_pages):
        r = jnp.maximum(_dot(iq, ikt_refs[k][0, 0].astype(BF16)), 0.0)
        sc_ref[0, :, k * page:(k + 1) * page] = jnp.sum(w * r, axis=0, keepdims=True)
    rn = jnp.maximum(jnp.sum(iq.astype(F32) * ikn_ref[0], axis=1, keepdims=True), 0.0)
    scn_ref[0] = jnp.broadcast_to(jnp.sum(w * rn, axis=0, keepdims=True), (1, LANES))


def _dec_index(pt_flat, layer, iq_d, iw_d, ik_new, cache_ikt):
    db = iq_d.shape[0]
    n_pages = pt_flat.shape[0] // db
    page = cache_ikt.shape[3]
    per_seq = lambda shape: pl.BlockSpec((1,) + shape, lambda b, pt: (b,) + (0,) * len(shape))
    grid_spec = pltpu.PrefetchScalarGridSpec(
        num_scalar_prefetch=1, grid=(db,),
        in_specs=[per_seq((IDX_HEADS, IDX_DIM)), per_seq((IDX_HEADS, 1)), per_seq((1, IDX_DIM))]
        + _paged_specs((1, 1, IDX_DIM, page), layer, n_pages),
        out_specs=[per_seq((1, n_pages * page)), per_seq((1, LANES))],
    )
    return pl.pallas_call(
        functools.partial(_dec_index_body, n_pages=n_pages, page=page), grid_spec=grid_spec, name="dec_index",
        out_shape=[jax.ShapeDtypeStruct((db, 1, n_pages * page), F32),
                   jax.ShapeDtypeStruct((db, 1, LANES), F32)],
        compiler_params=_cparams("parallel"),
    )(pt_flat, iq_d, iw_d, ik_new, *([cache_ikt] * n_pages))


def _dec_select_body(sc_ref, thr_ref, cut_ref, *, n_keys, k_sel):
    sc = sc_ref[...]
    col = lax.broadcasted_iota(I32, sc.shape, 1)
    keys = jnp.where(col < n_keys, _sortable(sc), INT_MIN)
    count = lambda m: jnp.sum(jnp.where(m, 1, 0).astype(I32), axis=1, keepdims=True)
    thr0 = jnp.where(count(keys >= 0) >= k_sel, 0, INT_MIN).astype(I32)

    def thr_step(b, thr):
        cand = thr + jnp.left_shift(jnp.int32(1), 30 - b)
        return jnp.where(count(keys >= cand) >= k_sel, cand, thr)

    thr = lax.fori_loop(0, 31, thr_step, thr0)
    need = k_sel - count(keys > thr)
    n_bits = (sc.shape[1] - 1).bit_length()

    def cut_step(b, cut):
        cand = cut + jnp.left_shift(jnp.int32(1), n_bits - 1 - b)
        return jnp.where(count((keys == thr) & (col < cand)) < need, cand, cut)

    thr_ref[...] = thr
    cut_ref[...] = lax.fori_loop(0, n_bits, cut_step, jnp.zeros(thr.shape, I32))


def _dec_select(scores, n_keys, k_sel):
    db = scores.shape[0]
    full = lambda shape: pl.BlockSpec(shape, lambda i: (0,) * len(shape))
    return pl.pallas_call(
        functools.partial(_dec_select_body, n_keys=n_keys, k_sel=k_sel), grid=(1,), name="dec_select",
        in_specs=[full(scores.shape)], out_specs=[full((db, 1)), full((db, 1))],
        out_shape=[jax.ShapeDtypeStruct((db, 1), I32), jax.ShapeDtypeStruct((db, 1), I32)],
        compiler_params=_cparams("arbitrary"),
    )(scores)


def _dec_attn_body(pt_ref, qa_ref, qb_ref, qc_ref, bsc_ref, bscn_ref, thr_ref, cut_ref,
                   akn_ref, avn_ref, lfn_ref, bkn_ref, bvn_ref, *rest, n_pages, page):
    groups = [rest[g * n_pages:(g + 1) * n_pages] for g in range(7)]
    akt, avt, lft, bkt, bvt, ckt, cvt = groups
    oa_ref, ob_ref, oc_ref = rest[7 * n_pages:]
    qa = qa_ref[0]
    qb = qb_ref[0]
    qc = qc_ref[0]
    thr = thr_ref[0]
    cut = cut_ref[0]
    tri = _strict_upper_tri(page)
    bf = lambda ref: ref[0, 0].astype(BF16)
    lanes = lambda xs: jnp.concatenate(xs, axis=1)

    def softmax_pv(logits, logit_n, v_pages, v_new):
        m = jnp.maximum(jnp.max(lanes(logits), axis=1, keepdims=True), logit_n)
        pn = jnp.exp2(logit_n - m)
        den = pn
        acc = pn * v_new
        for lg, vt in zip(logits, v_pages):
            p = jnp.exp2(lg - m)
            den = den + jnp.sum(p, axis=1, keepdims=True)
            acc = acc + _dot_nt(p.astype(BF16), bf(vt))
        return acc / den

    after = lfn_ref[0]
    logits = [None] * n_pages
    for k in reversed(range(n_pages)):
        lf = lft[k][0, 0]
        logits[k] = _dot(qa, bf(akt[k])) + (_dot_split(lf, tri) + after) * LOG2E
        after = after + jnp.sum(lf, axis=1, keepdims=True)
    logit_n = jnp.sum(qa.astype(F32) * akn_ref[0], axis=1, keepdims=True)
    oa = softmax_pv(logits, logit_n, avt, avn_ref[0])

    bsc = bsc_ref[0]
    pos = lax.broadcasted_iota(I32, bsc.shape, 1)
    sel = _selected(_sortable(bsc), thr, cut, pos)
    sel_n = _selected(_sortable(bscn_ref[0][:, 0:1]), thr, cut, jnp.int32(n_pages * page))
    logits = [jnp.where(sel[:, k * page:(k + 1) * page], _dot(qb, bf(bkt[k])), -jnp.inf) for k in range(n_pages)]
    logit_n = jnp.where(sel_n, jnp.sum(qb.astype(F32) * bkn_ref[0], axis=1, keepdims=True), -jnp.inf)
    ob = softmax_pv(logits, logit_n, bvt, bvn_ref[0])

    after = jnp.zeros((N_HEADS, 1), F32)
    oc = jnp.zeros((N_HEADS, KV_WIDTH), F32)
    for k in reversed(range(n_pages)):
        z = _dot(qc, bf(ckt[k]))
        sp = _softplus(z)
        a = jnp.exp(z - sp + _dot_split(-sp, tri) + after)
        oc = oc + _dot_nt(a.astype(BF16), bf(cvt[k]))
        after = after - jnp.sum(sp, axis=1, keepdims=True)

    hrow = lax.broadcasted_iota(I32, (N_HEADS, KV_WIDTH), 0)
    lane = lax.broadcasted_iota(I32, (N_HEADS, KV_WIDTH), 1)
    own = jnp.right_shift(lane, 6) == jnp.right_shift(hrow, 1)

    def own_head(acc):
        x = jnp.where(own, acc, 0.0)
        x = x[:, :LANES] + x[:, LANES:]
        return (x + pltpu.roll(x, HEAD_DIM, 1))[:, :HEAD_DIM]

    oa_ref[0] = own_head(oa)
    ob_ref[0] = own_head(ob)
    oc_ref[0] = own_head(oc)


def _dec_attention(pt_flat, layer, qa, qb, qc, caches_t, logf_t, bsc, bscn, thr, cut, new_rows):
    db = qa.shape[0]
    n_pages = pt_flat.shape[0] // db
    akt, avt, bkt, bvt, ckt, cvt = caches_t
    page = akt.shape[3]
    akn, avn, lfn, bkn, bvn = new_rows
    per_seq = lambda shape: pl.BlockSpec((1,) + shape, lambda b, pt: (b,) + (0,) * len(shape))
    kv_pages = lambda: _paged_specs((1, 1, KV_WIDTH, page), layer, n_pages)
    in_specs = [per_seq((N_HEADS, KV_WIDTH))] * 3 + [
        per_seq((1, n_pages * page)), per_seq((1, LANES)), per_seq((1, 1)), per_seq((1, 1)),
        per_seq((1, KV_WIDTH)), per_seq((1, KV_WIDTH)), per_seq((N_HEADS, 1)),
        per_seq((1, KV_WIDTH)), per_seq((1, KV_WIDTH)),
    ] + kv_pages() + kv_pages() + _paged_specs((1, 1, N_HEADS, page), layer, n_pages) \
      + kv_pages() + kv_pages() + kv_pages() + kv_pages()
    grid_spec = pltpu.PrefetchScalarGridSpec(
        num_scalar_prefetch=1, grid=(db,), in_specs=in_specs,
        out_specs=[per_seq((N_HEADS, HEAD_DIM))] * 3,
    )
    out = jax.ShapeDtypeStruct((db, N_HEADS, HEAD_DIM), F32)
    paged_args = []
    for arr in (akt, avt, logf_t, bkt, bvt, ckt, cvt):
        paged_args += [arr] * n_pages
    return pl.pallas_call(
        functools.partial(_dec_attn_body, n_pages=n_pages, page=page), grid_spec=grid_spec, name="dec_attn",
        out_shape=[out, out, out],
        compiler_params=_cparams("parallel"),
    )(pt_flat, qa, qb, qc, bsc, bscn, thr, cut, akn, avn, lfn, bkn, bvn, *paged_args)


def _rope_tables(pos):
    half = HEAD_DIM // 2
    inv_freq = ROPE_THETA ** (-jnp.arange(half, dtype=F32) / half)
    ang = pos.astype(F32)[:, None] * inv_freq[None, :]
    c = jnp.cos(ang)
    s = jnp.sin(ang)
    return jnp.tile(c, (1, 4)), jnp.tile(jnp.concatenate([-s, s], axis=1), (1, 2))


def _pad_cols(w, width):
    return jnp.pad(w, ((0, 0), (0, 0), (0, width - w.shape[-1])))


def _group_weights(w_in):
    splits = np.cumsum(PROJ_SIZES)[:-1].tolist()
    (aq, ak, av, az, af, bq, bk, bv, bz, biq, bik, biw, cq, ck, cv, cz, gates) = jnp.split(w_in, splits, axis=-1)
    wa = jnp.concatenate([aq, az, ak, av, _pad_cols(af, LANES)], axis=-1)
    wb = jnp.concatenate([bq, biq, bz, bk, bv, _pad_cols(bik, LANES), _pad_cols(biw, LANES)], axis=-1)
    wc = jnp.concatenate([cq, cz, ck, cv], axis=-1)
    return wa.astype(BF16), wb.astype(BF16), wc.astype(BF16), gates.astype(BF16)


def _rows_from_cols(qt):
    q = qt[0, :, :HEAD_DIM, :] + qt[0, :, HEAD_DIM:, :]
    return jnp.transpose(q, (2, 0, 1))


def _block_diag(q):
    own = (jnp.arange(N_HEADS)[:, None] // (N_HEADS // KV_HEADS)) == jnp.arange(KV_HEADS)[None, :]
    qbd = jnp.where(own[None, :, :, None], q[:, :, None, :], jnp.zeros((), q.dtype))
    return qbd.reshape(q.shape[0], N_HEADS, KV_WIDTH)


def _cache_rows(pa, pb, pc):
    def kv(a):
        b, _, s = a.shape
        return jnp.transpose(a.reshape(b, KV_HEADS, HEAD_DIM, s), (0, 3, 1, 2))
    rows = lambda a: jnp.transpose(a, (0, 2, 1))
    return (kv(pa["kt"]), kv(pa["vt"]), rows(pa["logft"]), kv(pb["kt"]), kv(pb["vt"]), rows(pb["ikt"]),
            kv(pc["kt"]), kv(pc["vt"]))


def _layer_prompt(x, lw, tabs, tiles, final_g, final_norm):
    b, s, _ = x.shape
    tm, tq_cols, tq_rows, tk = tiles
    x2d = x.reshape(b * s, D_MODEL)
    pa, pb, pc, sg = _project(x2d, b, s, tm, lw["norm_g"], lw["wa"], lw["wb"], lw["wc"], lw["wg"],
                              lw["bf"], tabs[0], tabs[1])
    dcum_t = _cumsum_heads(pa["logft"])
    oa = _fox_attention(pa["qt"], pa["k_hm"], pa["vt_bf"], dcum_t, tq_cols, tk)
    k_sel = max(1, min(TOPK_MAX, s // 4))
    keys, thr, cut, tie = _dsa_index(pb["iqt"], pb["iwt"], pb["ik_bf"].reshape(b, s, LANES),
                                     tq_rows, tq_rows, k_sel)
    ob = _dsa_attention(pb["qt"], pb["k_hm"], pb["vt_bf"], keys, thr, cut, tie, tq_cols, tk)
    oc = _sb_attention(pc["q_hm"], pc["k_hm"], pc["v_hm"], tq_rows, tk)
    out = _merge(oa.reshape(b * s, D_BRANCH), ob.reshape(b * s, D_BRANCH), oc.reshape(b * s, D_BRANCH),
                 pa["sz"], pb["sz"], pc["sz"], sg, x2d, lw["w_branch"], lw["w_out"], final_g, tm, final_norm)
    return out.reshape(b, s, D_MODEL), _cache_rows(pa, pb, pc)


def _layer_sample(x, layer, pt_flat, caches_t, logf_t, cache_ikt, lw, tabs, final_g, final_norm):
    db, t, _ = x.shape
    n_pages = pt_flat.shape[0] // db
    page = caches_t[0].shape[3]
    past = n_pages * page
    x2d = x.reshape(db * t, D_MODEL)
    pa, pb, pc, sg = _project(x2d, 1, db * t, db * t, lw["norm_g"], lw["wa"], lw["wb"], lw["wc"], lw["wg"],
                              lw["bf"], tabs[0], tabs[1])
    seq_rows = lambda a: jnp.transpose(a, (2, 0, 1))
    sc, scn = _dec_index(pt_flat, layer, _rows_from_cols(pb["iqt"]),
                         jnp.transpose(pb["iwt"], (2, 1, 0)), seq_rows(pb["ikt"]), cache_ikt)
    scores = jnp.concatenate([sc[:, 0, :], scn[:, 0, :]], axis=1)
    k_sel = max(1, min(TOPK_MAX, (past + t) // 4))
    thr, cut = _dec_select(scores, past + t, k_sel)
    new_rows = (seq_rows(pa["kt"]), seq_rows(pa["vt"]), jnp.transpose(pa["logft"], (2, 1, 0)),
                seq_rows(pb["kt"]), seq_rows(pb["vt"]))
    qc = pc["q_hm"][0, :, :, :HEAD_DIM].transpose(1, 0, 2)
    oa, ob, oc = _dec_attention(pt_flat, layer, _block_diag(_rows_from_cols(pa["qt"])),
                                _block_diag(_rows_from_cols(pb["qt"])), _block_diag(qc),
                                caches_t, logf_t, sc, scn, thr.reshape(db, 1, 1), cut.reshape(db, 1, 1), new_rows)
    out = _merge(oa.reshape(db, D_BRANCH), ob.reshape(db, D_BRANCH), oc.reshape(db, D_BRANCH),
                 pa["sz"], pb["sz"], pc["sz"], sg, x2d, lw["w_branch"], lw["w_out"], final_g, db * t, final_norm)
    new = tuple(jnp.swapaxes(a, 0, 1) for a in _cache_rows(pa, pb, pc))
    return out.reshape(db, t, D_MODEL), new


def _prompt_tiles(s):
    return min(512, s), min(1024, s), min(512, s), min(1024, s)


def kernel(x_prompt, x_sample, cache_a_k, cache_a_v, cache_a_logf, cache_b_k, cache_b_v, cache_b_idx_k,
           cache_c_k, cache_c_v, page_table, norm_g, w_in, b_forget, w_branch, w_out, final_g):
    depth = w_in.shape[0]
    b, s, _ = x_prompt.shape
    db, t, _ = x_sample.shape
    assert t == 1, "decode path handles one new token per sequence"
    n_pool, page = cache_a_k.shape[1], cache_a_k.shape[2]
    past = page_table.shape[1] * page

    wa, wb, wc, wg = _group_weights(w_in)
    bfp = jnp.pad(b_forget, ((0, 0), (0, LANES - N_HEADS))).reshape(depth, 1, LANES)
    wbr = w_branch.astype(BF16)
    wo = w_out.astype(BF16)
    tabs_p = _rope_tables(jnp.arange(s))
    tabs_s = _rope_tables(jnp.full((db,), past))
    page_t = lambda c: jnp.transpose(c, (0, 1, 3, 4, 2)).reshape(depth, n_pool, KV_WIDTH, page)
    caches_t = tuple(page_t(c) for c in (cache_a_k, cache_a_v, cache_b_k, cache_b_v, cache_c_k, cache_c_v))
    logf_t = jnp.transpose(cache_a_logf, (0, 1, 3, 2))
    cache_ikt = jnp.transpose(cache_b_idx_k, (0, 1, 3, 2))
    pt_flat = page_table.reshape(-1)

    xp, xs = x_prompt, x_sample
    p_rows, s_rows = [], []
    for l in range(depth):
        lw = dict(norm_g=norm_g[l], wa=wa[l], wb=wb[l], wc=wc[l], wg=wg[l], bf=bfp[l],
                  w_branch=wbr[l], w_out=wo[l])
        last = l == depth - 1
        xp, pn = _layer_prompt(xp, lw, tabs_p, _prompt_tiles(s), final_g, last)
        xs, sn = _layer_sample(xs, l, pt_flat, caches_t, logf_t, cache_ikt, lw, tabs_s, final_g, last)
        p_rows.append(pn)
        s_rows.append(sn)
    p_out = [jnp.stack([r[i] for r in p_rows]) for i in range(8)]
    s_out = [jnp.stack([r[i] for r in s_rows]) for i in range(8)]
    return (xp, xs, *p_out, *s_out)
```
